```python
import math
import jax, jax.numpy as jnp
from jax import lax
import numpy as np

D_MODEL = 1024
BATCH = 8
SEQ = 4096
DEPTH = 1
DEC_BATCH = 128
DEC_SEQ = 4
PAST_LEN = 8192
PAGE_SIZE = 128

MIX_WIDTH = D_MODEL
ATTN_WIDTH = MIX_WIDTH // 2
GMLP_WIDTH = MIX_WIDTH - ATTN_WIDTH
N_HEADS_A = 4
HEAD_DIM_A = ATTN_WIDTH // (2 * N_HEADS_A)
V_DIM_A = 2 * HEAD_DIM_A
QK_WIDTH = N_HEADS_A * 2 * HEAD_DIM_A
N_GROUPS_B = 4
GROUP_DIM_B = GMLP_WIDTH // N_GROUPS_B
CHUNK = 128
D_FF = 2816
CONV_W = 3
ROPE_THETA = 10000.0
NORM_EPS = 1e-6
Q_BLOCK = 128
NEG_INF = -1e30
PROJ_WIDTH = 2 * QK_WIDTH + ATTN_WIDTH + 2 * GMLP_WIDTH

kernel_name = "hymba_diffattn_chunkgmlp_convffn_step"


def rms_norm(x, g):
    xf = x.astype(jnp.float32)
    y = xf * lax.rsqrt(jnp.mean(xf * xf, axis=-1, keepdims=True) + NORM_EPS)
    return (y * g.astype(jnp.float32)).astype(x.dtype)


def layer_norm(x, g, b):
    xf = x.astype(jnp.float32)
    mu = jnp.mean(xf, axis=-1, keepdims=True)
    xc = xf - mu
    y = xc * lax.rsqrt(jnp.mean(xc * xc, axis=-1, keepdims=True) + NORM_EPS)
    return (y * g.astype(jnp.float32) + b.astype(jnp.float32)).astype(x.dtype)


def rope(x, pos):
    half = HEAD_DIM_A // 2
    inv = ROPE_THETA ** (-jnp.arange(half, dtype=jnp.float32) * 2.0 / HEAD_DIM_A)
    ang = pos[:, None] * inv[None, :]
    cos = jnp.cos(ang)[:, None, None, :]
    sin = jnp.sin(ang)[:, None, None, :]
    xf = x.astype(jnp.float32)
    x1, x2 = xf[..., :half], xf[..., half:]
    return jnp.concatenate([x1 * cos - x2 * sin, x2 * cos + x1 * sin], axis=-1).astype(x.dtype)


def diff_lambda(lq1, lk1, lq2, lk2, lam_init):
    f = jnp.float32
    return (jnp.exp(jnp.sum(lq1.astype(f) * lk1.astype(f)))
            - jnp.exp(jnp.sum(lq2.astype(f) * lk2.astype(f))) + lam_init)


def diff_weights(s, lam):
    p = jax.nn.softmax(s, axis=-1)
    return p[:, :, 0] - lam * p[:, :, 1]


def diff_attn_prompt(q, k, v, lam):
    B, S = q.shape[0], q.shape[1]
    nb = S // Q_BLOCK
    qb = q.reshape(B, nb, Q_BLOCK, N_HEADS_A, 2, HEAD_DIM_A).transpose(1, 0, 2, 3, 4, 5)
    kpos = jnp.arange(S)
    scale = HEAD_DIM_A ** -0.5

    def block(args):
        i, qi = args
        s = jnp.einsum('bqhsd,bkhsd->bhsqk', qi, k, preferred_element_type=jnp.float32) * scale
        qpos = i * Q_BLOCK + jnp.arange(Q_BLOCK)
        s = jnp.where(kpos[None, :] <= qpos[:, None], s, NEG_INF)
        w = diff_weights(s, lam)
        return jnp.einsum('bhqk,bkhe->bqhe', w.astype(v.dtype), v)

    o = lax.map(block, (jnp.arange(nb), qb))
    return o.transpose(1, 0, 2, 3, 4).reshape(B, S, N_HEADS_A, V_DIM_A)


def diff_attn_sample(q, k_new, v_new, k_past, v_past, lam):
    T = q.shape[1]
    P = k_past.shape[1]
    scale = HEAD_DIM_A ** -0.5
    s_past = jnp.einsum('bqhsd,bkhsd->bhsqk', q, k_past, preferred_element_type=jnp.float32) * scale
    s_new = jnp.einsum('bqhsd,bkhsd->bhsqk', q, k_new, preferred_element_type=jnp.float32) * scale
    causal = jnp.tril(jnp.ones((T, T), dtype=bool))
    s_new = jnp.where(causal, s_new, NEG_INF)
    w = diff_weights(jnp.concatenate([s_past, s_new], axis=-1), lam)
    return (jnp.einsum('bhqk,bkhe->bqhe', w[..., :P].astype(v_past.dtype), v_past)
            + jnp.einsum('bhqk,bkhe->bqhe', w[..., P:].astype(v_new.dtype), v_new))


def spatial_gate(u, gv, ln_g, ln_b, w_s, b_s):
    B, T = u.shape[0], u.shape[1]
    vn = layer_norm(gv, ln_g, ln_b)
    n = -(-T // CHUNK)
    vp = jnp.pad(vn, ((0, 0), (0, n * CHUNK - T), (0, 0))).reshape(B, n, CHUNK, N_GROUPS_B, GROUP_DIM_B)
    w = jnp.where(jnp.tril(jnp.ones((CHUNK, CHUNK), dtype=bool))[None], w_s, 0.0)
    s = jnp.einsum('gij,bnjgc->bnigc', w.astype(vp.dtype), vp) + b_s.T[None, None, :, :, None]
    s = s.reshape(B, n * CHUNK, GMLP_WIDTH)[:, :T]
    return u * s, vn


def mix_in(x, pos, g_pre, w_in_l):
    B, T = x.shape[0], x.shape[1]
    z = rms_norm(x, g_pre) @ w_in_l
    q, k, v, u, gv = jnp.split(z, [QK_WIDTH, 2 * QK_WIDTH, 2 * QK_WIDTH + ATTN_WIDTH,
                                   2 * QK_WIDTH + ATTN_WIDTH + GMLP_WIDTH], axis=-1)
    q = rope(q.reshape(B, T, N_HEADS_A, 2, HEAD_DIM_A), pos)
    k = rope(k.reshape(B, T, N_HEADS_A, 2, HEAD_DIM_A), pos)
    v = v.reshape(B, T, N_HEADS_A, V_DIM_A)
    return q, k, v, jax.nn.gelu(u), jax.nn.gelu(gv)


def mix_out(x, o, u, gv, subln_g, lam_init, ln_g, ln_b, w_s, b_s, w_out_l, g_post):
    B, T = x.shape[0], x.shape[1]
    o = rms_norm(o, subln_g) * (1.0 - lam_init)
    g, vn = spatial_gate(u, gv, ln_g, ln_b, w_s, b_s)
    m = jnp.concatenate([o.reshape(B, T, ATTN_WIDTH), g], axis=-1) @ w_out_l
    return x + rms_norm(m, g_post), vn


def ffn_block(x, prev, g_pre, w_up_l, conv_w_l, conv_b_l, w_down_l, g_post):
    up = rms_norm(x, g_pre) @ w_up_l
    T = up.shape[1]
    xp = jnp.concatenate([prev.astype(up.dtype), up], axis=1)
    c = conv_b_l + conv_w_l[0] * xp[:, 0:T]
    for j in range(1, CONV_W):
        c = c + conv_w_l[j] * xp[:, j:j + T]
    gate, val = jnp.split(c, 2, axis=-1)
    y = (jax.nn.gelu(gate) * val) @ w_down_l
    return x + rms_norm(y, g_post), xp[:, T:]


def setup_inputs(seed: int = 0) -> dict:
    key = jax.random.key(seed)
    ks = jax.random.split(key, 32)
    f32 = jnp.float32
    n_pages = PAST_LEN // PAGE_SIZE
    n_phys = (DEC_BATCH * n_pages * 5 + 3) // 4

    def nrm(k, shape, s):
        return jax.random.normal(k, shape, f32) * s

    L = DEPTH
    return {
        "x_prompt": nrm(ks[0], (BATCH, SEQ, D_MODEL), 1.0),
        "x_sample": nrm(ks[1], (DEC_BATCH, DEC_SEQ, D_MODEL), 1.0),
        "cache_k": nrm(ks[2], (L, n_phys, PAGE_SIZE, N_HEADS_A, 2, HEAD_DIM_A), 1.0),
        "cache_v": nrm(ks[3], (L, n_phys, PAGE_SIZE, N_HEADS_A, V_DIM_A), 1.0),
        "state_conv": nrm(ks[4], (L, DEC_BATCH, CONV_W - 1, 2 * D_FF), 1.0),
        "page_table": jax.random.permutation(ks[5], n_phys)[:DEC_BATCH * n_pages]
                        .reshape(DEC_BATCH, n_pages).astype(jnp.int32),
        "norm_mix_pre": 1.0 + nrm(ks[6], (L, D_MODEL), 0.05),
        "w_in": nrm(ks[7], (L, D_MODEL, PROJ_WIDTH), D_MODEL ** -0.5),
        "lambda_q1": nrm(ks[8], (L, HEAD_DIM_A), 0.1),
        "lambda_k1": nrm(ks[9], (L, HEAD_DIM_A), 0.1),
        "lambda_q2": nrm(ks[10], (L, HEAD_DIM_A), 0.1),
        "lambda_k2": nrm(ks[11], (L, HEAD_DIM_A), 0.1),
        "subln_g": 1.0 + nrm(ks[12], (L, V_DIM_A), 0.05),
        "gate_ln_g": 1.0 + nrm(ks[13], (L, GMLP_WIDTH), 0.05),
        "gate_ln_b": nrm(ks[14], (L, GMLP_WIDTH), 0.02),
        "w_spatial": nrm(ks[15], (L, N_GROUPS_B, CHUNK, CHUNK), CHUNK ** -0.5),
        "b_spatial": 1.0 + nrm(ks[16], (L, N_GROUPS_B, CHUNK), 0.1),
        "w_out": nrm(ks[17], (L, MIX_WIDTH, D_MODEL), MIX_WIDTH ** -0.5),
        "norm_mix_post": 1.0 + nrm(ks[18], (L, D_MODEL), 0.05),
        "norm_ffn_pre": 1.0 + nrm(ks[19], (L, D_MODEL), 0.05),
        "w_up": nrm(ks[20], (L, D_MODEL, 2 * D_FF), D_MODEL ** -0.5),
        "conv_w": nrm(ks[21], (L, CONV_W, 2 * D_FF), CONV_W ** -0.5),
        "conv_b": nrm(ks[22], (L, 2 * D_FF), 0.01),
        "w_down": nrm(ks[23], (L, D_FF, D_MODEL), D_FF ** -0.5),
        "norm_ffn_post": 1.0 + nrm(ks[24], (L, D_MODEL), 0.05),
    }


def reference(x_prompt, x_sample, cache_k, cache_v, state_conv, page_table, norm_mix_pre, w_in,
              lambda_q1, lambda_k1, lambda_q2, lambda_k2, subln_g, gate_ln_g, gate_ln_b, w_spatial,
              b_spatial, w_out, norm_mix_post, norm_ffn_pre, w_up, conv_w, conv_b, w_down,
              norm_ffn_post):
    n_dec = x_sample.shape[0]
    past_len = page_table.shape[1] * cache_k.shape[2]
    pos_p = jnp.arange(x_prompt.shape[1], dtype=jnp.float32)
    pos_s = jnp.arange(x_sample.shape[1], dtype=jnp.float32) + float(past_len)
    xp, xs = x_prompt, x_sample
    kp_l, vp_l, cp_l, ks_l, vs_l, cs_l, gs_l = [], [], [], [], [], [], []
    for l in range(DEPTH):
        lam_init = 0.8 - 0.6 * math.exp(-0.3 * l)
        lam = diff_lambda(lambda_q1[l], lambda_k1[l], lambda_q2[l], lambda_k2[l], lam_init)
        out_args = (subln_g[l], lam_init, gate_ln_g[l], gate_ln_b[l], w_spatial[l], b_spatial[l],
                    w_out[l], norm_mix_post[l])
        ffn_args = (norm_ffn_pre[l], w_up[l], conv_w[l], conv_b[l], w_down[l], norm_ffn_post[l])

        q, k, v, u, gv = mix_in(xp, pos_p, norm_mix_pre[l], w_in[l])
        o = diff_attn_prompt(q, k, v, lam)
        xp, _ = mix_out(xp, o, u, gv, *out_args)
        zeros_prev = jnp.zeros((xp.shape[0], CONV_W - 1, 2 * D_FF), xp.dtype)
        xp, conv_p = ffn_block(xp, zeros_prev, *ffn_args)
        kp_l.append(k); vp_l.append(v); cp_l.append(conv_p)

        q, k, v, u, gv = mix_in(xs, pos_s, norm_mix_pre[l], w_in[l])
        k_past = cache_k[l, page_table].reshape(n_dec, past_len, N_HEADS_A, 2, HEAD_DIM_A)
        v_past = cache_v[l, page_table].reshape(n_dec, past_len, N_HEADS_A, V_DIM_A)
        o = diff_attn_sample(q, k, v, k_past, v_past, lam)
        xs, vn_s = mix_out(xs, o, u, gv, *out_args)
        xs, conv_s = ffn_block(xs, state_conv[l], *ffn_args)
        ks_l.append(k); vs_l.append(v); cs_l.append(conv_s); gs_l.append(vn_s)

    return (xp, xs, jnp.stack(kp_l), jnp.stack(vp_l), jnp.stack(cp_l),
            jnp.stack(ks_l), jnp.stack(vs_l), jnp.stack(cs_l), jnp.stack(gs_l))
```

```python
import functools
import math

import jax
import jax.numpy as jnp
from jax import lax
from jax.experimental import pallas as pl
from jax.experimental.pallas import tpu as pltpu

F32 = jnp.float32
BF16 = jnp.bfloat16

HEAD_DIM = 64
V_DIM = 2 * HEAD_DIM
CHUNK = 128
N_GROUPS = 4
CONV_W = 3
ROPE_THETA = 10000.0
NORM_EPS = 1e-6
NEG_INF = -1e30
LANES = 128
VMEM_LIMIT = 56 * 1024 * 1024


def _rms(x, g):
    return x * lax.rsqrt(jnp.mean(x * x, axis=-1, keepdims=True) + NORM_EPS) * g


def _gelu(x):
    c = math.sqrt(2.0 / math.pi)
    return x * (0.5 * (1.0 + jnp.tanh(c * (x + 0.044715 * (x * x * x)))))


def _dot(a, b):
    return jnp.dot(a, b, preferred_element_type=F32)


def _dot_nt(a, b):
    return lax.dot_general(a, b, (((1,), (1,)), ((), ())), preferred_element_type=F32)


def _diff_lambda(lq1, lk1, lq2, lk2, lam_init):
    a = jnp.sum(lq1 * lk1, axis=-1, keepdims=True)
    b = jnp.sum(lq2 * lk2, axis=-1, keepdims=True)
    return jnp.exp(a) - jnp.exp(b) + lam_init


def _rope(z, cos, sin_signed, lo_half):
    outs = []
    for c in range(z.shape[1] // LANES):
        xs = z[:, c * LANES:(c + 1) * LANES]
        ahead = pltpu.roll(xs, LANES - HEAD_DIM // 2, 1)
        behind = pltpu.roll(xs, HEAD_DIM // 2, 1)
        outs.append(xs * cos + jnp.where(lo_half, ahead, behind) * sin_signed)
    return jnp.concatenate(outs, axis=1)


def _mix_in_body(x_ref, g_ref, w_ref, cos_ref, sin_ref, lng_ref, lnb_ref,
                 q_ref, kf_ref, kb_ref, vf_ref, vb_ref, u_ref, vn_ref, *, qk_w, attn_w, gm_w):
    h = _rms(x_ref[...], g_ref[...]).astype(BF16)
    cos = cos_ref[...]
    sin_signed = sin_ref[...]
    lane = lax.broadcasted_iota(jnp.int32, (1, LANES), 1)
    lo_half = (lane % HEAD_DIM) < (HEAD_DIM // 2)

    c0 = 0
    q = _rope(_dot(h, w_ref[:, c0:c0 + qk_w]), cos, sin_signed, lo_half)
    q_ref[...] = (q * (HEAD_DIM ** -0.5)).astype(BF16)
    c0 += qk_w
    k = _rope(_dot(h, w_ref[:, c0:c0 + qk_w]), cos, sin_signed, lo_half)
    kf_ref[...] = k
    kb_ref[...] = k.astype(BF16)
    c0 += qk_w
    v = _dot(h, w_ref[:, c0:c0 + attn_w])
    vf_ref[...] = v
    vb_ref[...] = v.astype(BF16)
    c0 += attn_w
    u_ref[...] = _gelu(_dot(h, w_ref[:, c0:c0 + gm_w]))
    c0 += gm_w
    gv = _gelu(_dot(h, w_ref[:, c0:c0 + gm_w]))
    mu = jnp.mean(gv, axis=-1, keepdims=True)
    gc = gv - mu
    vn = gc * lax.rsqrt(jnp.mean(gc * gc, axis=-1, keepdims=True) + NORM_EPS)
    vn_ref[...] = vn * lng_ref[...] + lnb_ref[...]


def _mix_in(x2d, g_pre, w_in_bf, cos_t, sin_t, ln_g, ln_b, *, tm, row_map, tab_map, qk_w, attn_w, gm_w,
            n_rows, grid):
    d_model = w_in_bf.shape[0]
    proj_w = w_in_bf.shape[1]
    const = lambda i: (0, 0)
    n_col_blocks = x2d.shape[1] // d_model

    def out(width, dtype):
        return (jax.ShapeDtypeStruct((n_rows, n_col_blocks * width), dtype),
                pl.BlockSpec((tm, width), row_map))

    outs = [out(qk_w, BF16), out(qk_w, F32), out(qk_w, BF16), out(attn_w, F32), out(attn_w, BF16),
            out(gm_w, F32), out(gm_w, F32)]
    return pl.pallas_call(
        functools.partial(_mix_in_body, qk_w=qk_w, attn_w=attn_w, gm_w=gm_w),
        grid=grid,
        in_specs=[
            pl.BlockSpec((tm, d_model), row_map),
            pl.BlockSpec((1, d_model), const),
            pl.BlockSpec((d_model, proj_w), const),
            pl.BlockSpec((tm, LANES), tab_map),
            pl.BlockSpec((tm, LANES), tab_map),
            pl.BlockSpec((1, gm_w), const),
            pl.BlockSpec((1, gm_w), const),
        ],
        out_specs=[o[1] for o in outs],
        out_shape=[o[0] for o in outs],
        compiler_params=pltpu.CompilerParams(dimension_semantics=("parallel",),
                                             vmem_limit_bytes=VMEM_LIMIT),
        name="mix_in",
    )(x2d, g_pre, w_in_bf, cos_t, sin_t, ln_g, ln_b)


def _prompt_attn_body(qi_ref, kj_ref, q_ref, k_ref, v_ref, lq1_ref, lk1_ref, lq2_ref, lk2_ref, sg_ref,
                      o_ref, m1, l1, a1, m2, l2, a2, *, lam_init):
    p = pl.program_id(2)
    qi = qi_ref[p]
    kj = kj_ref[p]
    tq = q_ref.shape[1]
    tk = k_ref.shape[1]

    @pl.when(kj == 0)
    def _():
        m1[...] = jnp.full(m1.shape, NEG_INF, F32)
        m2[...] = jnp.full(m2.shape, NEG_INF, F32)
        l1[...] = jnp.zeros(l1.shape, F32)
        l2[...] = jnp.zeros(l2.shape, F32)
        a1[...] = jnp.zeros(a1.shape, F32)
        a2[...] = jnp.zeros(a2.shape, F32)

    q = q_ref[0]
    k = k_ref[0]
    v = v_ref[0]
    lane = lax.broadcasted_iota(jnp.int32, (1, V_DIM), 1)
    zero = jnp.zeros_like(q)
    s1 = _dot_nt(jnp.where(lane < HEAD_DIM, q, zero), k)
    s2 = _dot_nt(jnp.where(lane >= HEAD_DIM, q, zero), k)

    def update(s, m_ref, l_ref, a_ref):
        m_prev = m_ref[...]
        m_new = jnp.maximum(m_prev, jnp.max(s, axis=-1, keepdims=True))
        alpha = jnp.exp(m_prev - m_new)
        pr = jnp.exp(s - m_new)
        l_ref[...] = alpha * l_ref[...] + jnp.sum(pr, axis=-1, keepdims=True)
        a_ref[...] = alpha * a_ref[...] + _dot(pr.astype(BF16), v)
        m_ref[...] = m_new

    @pl.when(kj < qi)
    def _():
        update(s1, m1, l1, a1)
        update(s2, m2, l2, a2)

    @pl.when(kj == qi)
    def _():
        row = lax.broadcasted_iota(jnp.int32, (tq, tk), 0)
        col = lax.broadcasted_iota(jnp.int32, (tq, tk), 1)
        keep = col <= row
        update(jnp.where(keep, s1, NEG_INF), m1, l1, a1)
        update(jnp.where(keep, s2, NEG_INF), m2, l2, a2)
        lam = _diff_lambda(lq1_ref[...], lk1_ref[...], lq2_ref[...], lk2_ref[...], lam_init)
        o = a1[...] / l1[...] - lam * (a2[...] / l2[...])
        o_ref[0] = _rms(o, sg_ref[...]) * (1.0 - lam_init)


def _prompt_attn(q, k, v, lq1, lk1, lq2, lk2, subln_g, *, lam_init, n_heads, tq):
    B, S, _ = q.shape
    nq = S // tq
    pairs = [(i, j) for i in range(nq) for j in range(i + 1)]
    qi = jnp.asarray([p[0] for p in pairs], jnp.int32)
    kj = jnp.asarray([p[1] for p in pairs], jnp.int32)
    const = lambda b, h, p, qi, kj: (0, 0)
    grid_spec = pltpu.PrefetchScalarGridSpec(
        num_scalar_prefetch=2,
        grid=(B, n_heads, len(pairs)),
        in_specs=[
            pl.BlockSpec((1, tq, V_DIM), lambda b, h, p, qi, kj: (b, qi[p], h)),
            pl.BlockSpec((1, tq, V_DIM), lambda b, h, p, qi, kj: (b, kj[p], h)),
            pl.BlockSpec((1, tq, V_DIM), lambda b, h, p, qi, kj: (b, kj[p], h)),
            pl.BlockSpec((1, HEAD_DIM), const),
            pl.BlockSpec((1, HEAD_DIM), const),
            pl.BlockSpec((1, HEAD_DIM), const),
            pl.BlockSpec((1, HEAD_DIM), const),
            pl.BlockSpec((1, V_DIM), const),
        ],
        out_specs=pl.BlockSpec((1, tq, V_DIM), lambda b, h, p, qi, kj: (b, qi[p], h)),
        scratch_shapes=[
            pltpu.VMEM((tq, 1), F32), pltpu.VMEM((tq, 1), F32), pltpu.VMEM((tq, V_DIM), F32),
            pltpu.VMEM((tq, 1), F32), pltpu.VMEM((tq, 1), F32), pltpu.VMEM((tq, V_DIM), F32),
        ],
    )
    return pl.pallas_call(
        functools.partial(_prompt_attn_body, lam_init=lam_init),
        grid_spec=grid_spec,
        out_shape=jax.ShapeDtypeStruct((B, S, n_heads * V_DIM), F32),
        compiler_params=pltpu.CompilerParams(
            dimension_semantics=("parallel", "parallel", "arbitrary"), vmem_limit_bytes=VMEM_LIMIT),
        name="prompt_attn",
    )(qi, kj, q, k, v, lq1, lk1, lq2, lk2, subln_g)


def _paged_attn_body(pt_ref, qbd_ref, kn_ref, vn_ref, lq1_ref, lk1_ref, lq2_ref, lk2_ref, sg_ref, *rest,
                     lam_init, n_pages_step, n_heads, n_new):
    k_refs = rest[:n_pages_step]
    v_refs = rest[n_pages_step:2 * n_pages_step]
    o_ref = rest[2 * n_pages_step]
    m_sc, l_sc, acc_sc, y_sc = rest[2 * n_pages_step + 1:]
    g = pl.program_id(1)
    n_rows = qbd_ref.shape[1]
    half = n_rows // 2

    @pl.when(g == 0)
    def _():
        m_sc[...] = jnp.full(m_sc.shape, NEG_INF, F32)
        l_sc[...] = jnp.zeros(l_sc.shape, F32)
        acc_sc[...] = jnp.zeros(acc_sc.shape, F32)

    qbd = qbd_ref[0]
    s = jnp.concatenate([_dot(qbd, r[0].astype(BF16)) for r in k_refs], axis=1)
    m_prev = m_sc[...]
    m_new = jnp.maximum(m_prev, jnp.max(s, axis=-1, keepdims=True))
    alpha = jnp.exp(m_prev - m_new)
    pr = jnp.exp(s - m_new)
    l_sc[...] = alpha * l_sc[...] + jnp.sum(pr, axis=-1, keepdims=True)
    prb = pr.astype(BF16)
    for h in range(n_heads):
        cols = slice(h * V_DIM, (h + 1) * V_DIM)
        v_h = jnp.concatenate([r[0, :, h, :] for r in v_refs], axis=0).astype(BF16)
        acc_sc[:, cols] = alpha * acc_sc[:, cols] + _dot(prb, v_h)
    m_sc[...] = m_new

    @pl.when(g == pl.num_programs(1) - 1)
    def _():
        qf = qbd.astype(F32)
        q_tok = lax.broadcasted_iota(jnp.int32, (n_rows, 1), 0) % n_new
        s_new = []
        for j in range(n_new):
            kj = kn_ref[0, j:j + 1, :].astype(BF16).astype(F32)
            s_new.append(jnp.where(q_tok >= j, jnp.sum(qf * kj, axis=-1, keepdims=True), NEG_INF))
        m_prev = m_sc[...]
        m_fin = m_prev
        for sj in s_new:
            m_fin = jnp.maximum(m_fin, sj)
        alpha = jnp.exp(m_prev - m_fin)
        l_fin = alpha * l_sc[...]
        acc = alpha * acc_sc[...]
        for j, sj in enumerate(s_new):
            pj = jnp.exp(sj - m_fin)
            l_fin = l_fin + pj
            acc = acc + pj.astype(BF16).astype(F32) * vn_ref[0, j:j + 1, :].astype(BF16).astype(F32)
        lam = _diff_lambda(lq1_ref[...], lk1_ref[...], lq2_ref[...], lk2_ref[...], lam_init)
        o = acc / l_fin
        o = o[:half] - lam * o[half:]
        for h in range(n_heads):
            blk = o[:, h * V_DIM:(h + 1) * V_DIM]
            y_sc[:, h * V_DIM:(h + 1) * V_DIM] = _rms(blk, sg_ref[...]) * (1.0 - lam_init)
        for h in range(n_heads):
            o_ref[0, :, h * V_DIM:(h + 1) * V_DIM] = y_sc[h * n_new:(h + 1) * n_new, h * V_DIM:(h + 1) * V_DIM]


def _paged_attn(page_table, qbd, k_new, v_new, cache_kt, cache_v, lq1, lk1, lq2, lk2, subln_g, *,
                lam_init, n_heads, n_pages_step):
    n_dec, n_pages = page_table.shape
    _, width, page = cache_kt.shape
    n_new = k_new.shape[1]
    n_rows = qbd.shape[1]
    G = n_pages_step
    const = lambda b, g, pt: (0, 0)
    per_b = lambda b, g, pt: (b, 0, 0)

    def k_spec(j):
        return pl.BlockSpec((1, width, page), lambda b, g, pt: (pt[b, g * G + j], 0, 0))

    def v_spec(j):
        return pl.BlockSpec((1, page, n_heads, V_DIM), lambda b, g, pt: (pt[b, g * G + j], 0, 0, 0))

    grid_spec = pltpu.PrefetchScalarGridSpec(
        num_scalar_prefetch=1,
        grid=(n_dec, n_pages // G),
        in_specs=[
            pl.BlockSpec((1, n_rows, width), per_b),
            pl.BlockSpec((1, n_new, width), per_b),
            pl.BlockSpec((1, n_new, width), per_b),
            pl.BlockSpec((1, HEAD_DIM), const),
            pl.BlockSpec((1, HEAD_DIM), const),
            pl.BlockSpec((1, HEAD_DIM), const),
            pl.BlockSpec((1, HEAD_DIM), const),
            pl.BlockSpec((1, V_DIM), const),
        ] + [k_spec(j) for j in range(G)] + [v_spec(j) for j in range(G)],
        out_specs=pl.BlockSpec((1, n_new, width), per_b),
        scratch_shapes=[
            pltpu.VMEM((n_rows, 1), F32), pltpu.VMEM((n_rows, 1), F32), pltpu.VMEM((n_rows, width), F32),
            pltpu.VMEM((n_rows // 2, width), F32),
        ],
    )
    return pl.pallas_call(
        functools.partial(_paged_attn_body, lam_init=lam_init, n_pages_step=G, n_heads=n_heads, n_new=n_new),
        grid_spec=grid_spec,
        out_shape=jax.ShapeDtypeStruct((n_dec, n_new, width), F32),
        compiler_params=pltpu.CompilerParams(
            dimension_semantics=("parallel", "arbitrary"), vmem_limit_bytes=VMEM_LIMIT),
        name="paged_attn",
    )(page_table, qbd, k_new, v_new, lq1, lk1, lq2, lk2, subln_g, *([cache_kt] * G), *([cache_v] * G))


def _mix_out_prompt_body(x_ref, o_ref, u_ref, vn_ref, ws_ref, bs_ref, wout_ref, gpost_ref, y_ref, gate_sc):
    tm = x_ref.shape[0]
    attn_w = o_ref.shape[1]
    gd = vn_ref.shape[1] // N_GROUPS
    row = lax.broadcasted_iota(jnp.int32, (CHUNK, CHUNK), 0)
    col = lax.broadcasted_iota(jnp.int32, (CHUNK, CHUNK), 1)
    for g in range(N_GROUPS):
        wg = jnp.where(col <= row, ws_ref[g], 0.0).astype(BF16)
        bias = bs_ref[:, g * gd:(g + 1) * gd]
        for c in range(tm // CHUNK):
            rows = slice(c * CHUNK, (c + 1) * CHUNK)
            cols = slice(g * gd, (g + 1) * gd)
            s = _dot(wg, vn_ref[rows, cols].astype(BF16)) + bias
            gate_sc[rows, cols] = (u_ref[rows, cols] * s).astype(BF16)
    m = _dot(o_ref[...].astype(BF16), wout_ref[:attn_w, :]) + _dot(gate_sc[...], wout_ref[attn_w:, :])
    y_ref[...] = x_ref[...] + _rms(m, gpost_ref[...])


def _mix_out_prompt(x2d, o2d, u2d, vn2d, w_spatial, bs_exp, w_out_bf, g_post, *, tm):
    n_rows, d_model = x2d.shape
    attn_w = o2d.shape[1]
    gm_w = u2d.shape[1]
    const2 = lambda i: (0, 0)
    rows = lambda i: (i, 0)
    return pl.pallas_call(
        _mix_out_prompt_body,
        grid=(n_rows // tm,),
        in_specs=[
            pl.BlockSpec((tm, d_model), rows),
            pl.BlockSpec((tm, attn_w), rows),
            pl.BlockSpec((tm, gm_w), rows),
            pl.BlockSpec((tm, gm_w), rows),
            pl.BlockSpec(w_spatial.shape, lambda i: (0, 0, 0)),
            pl.BlockSpec(bs_exp.shape, const2),
            pl.BlockSpec(w_out_bf.shape, const2),
            pl.BlockSpec((1, d_model), const2),
        ],
        out_specs=pl.BlockSpec((tm, d_model), rows),
        out_shape=jax.ShapeDtypeStruct((n_rows, d_model), F32),
        scratch_shapes=[pltpu.VMEM((tm, gm_w), BF16)],
        compiler_params=pltpu.CompilerParams(dimension_semantics=("parallel",),
                                             vmem_limit_bytes=VMEM_LIMIT),
        name="mix_out_prompt",
    )(x2d, o2d, u2d, vn2d, w_spatial, bs_exp, w_out_bf, g_post)


def _mix_out_sample_body(x_ref, o_ref, u_ref, vn_ref, coef_ref, bias_ref, wout_ref, gpost_ref, y_ref, hist):
    t = pl.program_id(0)
    n_t = hist.shape[0]
    attn_w = o_ref.shape[1]

    @pl.when(t == 0)
    def _():
        hist[...] = jnp.zeros(hist.shape, F32)

    hist[t] = vn_ref[...]
    s = bias_ref[pl.ds(t, 1), :]
    for j in range(n_t):
        cj = jnp.where(j <= t, coef_ref[pl.ds(t * n_t + j, 1), :], 0.0)
        s = s + cj.astype(BF16).astype(F32) * hist[j].astype(BF16).astype(F32)
    gate = (u_ref[...] * s).astype(BF16)
    m = _dot(o_ref[...].astype(BF16), wout_ref[:attn_w, :]) + _dot(gate, wout_ref[attn_w:, :])
    y_ref[...] = x_ref[...] + _rms(m, gpost_ref[...])


def _mix_out_sample(x2d, o2d, u2d, vn2d, coef, bias, w_out_bf, g_post, *, n_t):
    n_dec = x2d.shape[0]
    d_model = x2d.shape[1] // n_t
    attn_w = o2d.shape[1] // n_t
    gm_w = u2d.shape[1] // n_t
    const2 = lambda t: (0, 0)
    colblk = lambda t: (0, t)
    return pl.pallas_call(
        _mix_out_sample_body,
        grid=(n_t,),
        in_specs=[
            pl.BlockSpec((n_dec, d_model), colblk),
            pl.BlockSpec((n_dec, attn_w), colblk),
            pl.BlockSpec((n_dec, gm_w), colblk),
            pl.BlockSpec((n_dec, gm_w), colblk),
            pl.BlockSpec(coef.shape, const2),
            pl.BlockSpec(bias.shape, const2),
            pl.BlockSpec(w_out_bf.shape, const2),
            pl.BlockSpec((1, d_model), const2),
        ],
        out_specs=pl.BlockSpec((n_dec, d_model), colblk),
        out_shape=jax.ShapeDtypeStruct(x2d.shape, F32),
        scratch_shapes=[pltpu.VMEM((n_t, n_dec, gm_w), F32)],
        compiler_params=pltpu.CompilerParams(dimension_semantics=("arbitrary",),
                                             vmem_limit_bytes=VMEM_LIMIT),
        name="mix_out_sample",
    )(x2d, o2d, u2d, vn2d, coef, bias, w_out_bf, g_post)


def _ffn_tail(x, conv_rows, cw_ref, cb_ref, wdn_ref, gpost_ref, *, d_ff, col_chunk):
    acc = None
    for c in range(d_ff // col_chunk):
        def conv(cols):
            y = cb_ref[:, cols] + cw_ref[0:1, cols] * conv_rows(0, cols)
            for j in range(1, CONV_W):
                y = y + cw_ref[j:j + 1, cols] * conv_rows(j, cols)
            return y
        gate = conv(slice(c * col_chunk, (c + 1) * col_chunk))
        val = conv(slice(d_ff + c * col_chunk, d_ff + (c + 1) * col_chunk))
        part = _dot((_gelu(gate) * val).astype(BF16), wdn_ref[c * col_chunk:(c + 1) * col_chunk, :])
        acc = part if acc is None else acc + part
    return x + _rms(acc, gpost_ref[...])


def _ffn_prompt_body(x_ref, g_ref, wup_ref, cw_ref, cb_ref, wdn_ref, gpost_ref, y_ref, conv_ref, upbuf, *,
                     d_ff, col_chunk):
    j = pl.program_id(1)
    tm = x_ref.shape[1]
    head = 8
    lo = head - (CONV_W - 1)

    @pl.when(j == 0)
    def _():
        upbuf[lo:head, :] = jnp.zeros((CONV_W - 1, upbuf.shape[1]), F32)

    @pl.when(j > 0)
    def _():
        upbuf[lo:head, :] = upbuf[tm + lo:tm + head, :]

    x = x_ref[0]
    h = _rms(x, g_ref[...]).astype(BF16)
    up_chunk = 512
    for c in range(2 * d_ff // up_chunk):
        cols = slice(c * up_chunk, (c + 1) * up_chunk)
        upbuf[head:head + tm, cols] = _dot(h, wup_ref[:, cols])

    @pl.when(j == pl.num_programs(1) - 1)
    def _():
        conv_ref[0] = upbuf[tm + lo:tm + head, :]

    y_ref[0] = _ffn_tail(x, lambda tap, cols: upbuf[lo + tap:lo + tap + tm, cols],
                         cw_ref, cb_ref, wdn_ref, gpost_ref, d_ff=d_ff, col_chunk=col_chunk)


def _ffn_prompt(x3d, g_pre, w_up_bf, conv_w, conv_b, w_down_bf, g_post, *, tm, col_chunk):
    B, S, d_model = x3d.shape
    d_ff = w_down_bf.shape[0]
    const2 = lambda b, j: (0, 0)
    tile = lambda b, j: (b, j, 0)
    return pl.pallas_call(
        functools.partial(_ffn_prompt_body, d_ff=d_ff, col_chunk=col_chunk),
        grid=(B, S // tm),
        in_specs=[
            pl.BlockSpec((1, tm, d_model), tile),
            pl.BlockSpec((1, d_model), const2),
            pl.BlockSpec(w_up_bf.shape, const2, pipeline_mode=pl.Buffered(1)),
            pl.BlockSpec(conv_w.shape, const2),
            pl.BlockSpec(conv_b.shape, const2),
            pl.BlockSpec(w_down_bf.shape, const2, pipeline_mode=pl.Buffered(1)),
            pl.BlockSpec((1, d_model), const2),
        ],
        out_specs=[
            pl.BlockSpec((1, tm, d_model), tile),
            pl.BlockSpec((1, CONV_W - 1, 2 * d_ff), lambda b, j: (b, 0, 0)),
        ],
        out_shape=[
            jax.ShapeDtypeStruct((B, S, d_model), F32),
            jax.ShapeDtypeStruct((B, CONV_W - 1, 2 * d_ff), F32),
        ],
        scratch_shapes=[pltpu.VMEM((8 + tm, 2 * d_ff), F32)],
        compiler_params=pltpu.CompilerParams(dimension_semantics=("parallel", "arbitrary"),
                                             vmem_limit_bytes=VMEM_LIMIT),
        name="ffn_prompt",
    )(x3d, g_pre, w_up_bf, conv_w, conv_b, w_down_bf, g_post)


def _ffn_sample_body(x_ref, st_ref, g_ref, wup_ref, cw_ref, cb_ref, wdn_ref, gpost_ref, y_ref, conv_ref, ring, *,
                     d_ff, col_chunk):
    t = pl.program_id(0)
    width = 2 * d_ff

    @pl.when(t == 0)
    def _():
        for j in range(CONV_W - 1):
            ring[j] = st_ref[:, j * width:(j + 1) * width]

    x = x_ref[...]
    h = _rms(x, g_ref[...]).astype(BF16)
    cur = (t + CONV_W - 1) % CONV_W
    up_chunk = 512
    for c in range(width // up_chunk):
        cols = slice(c * up_chunk, (c + 1) * up_chunk)
        up = _dot(h, wup_ref[:, cols])
        ring[cur, :, cols] = up
        conv_ref[:, cols] = up

    y_ref[...] = _ffn_tail(x, lambda tap, cols: ring[(t + tap) % CONV_W, :, cols],
                           cw_ref, cb_ref, wdn_ref, gpost_ref, d_ff=d_ff, col_chunk=col_chunk)


def _ffn_sample(x2d, state2d, g_pre, w_up_bf, conv_w, conv_b, w_down_bf, g_post, *, n_t, col_chunk):
    n_dec = x2d.shape[0]
    d_model = x2d.shape[1] // n_t
    d_ff = w_down_bf.shape[0]
    width = 2 * d_ff
    const2 = lambda t: (0, 0)
    colblk = lambda t: (0, t)
    conv_map = lambda t: (0, jnp.maximum(t - (n_t - (CONV_W - 1)), 0))
    return pl.pallas_call(
        functools.partial(_ffn_sample_body, d_ff=d_ff, col_chunk=col_chunk),
        grid=(n_t,),
        in_specs=[
            pl.BlockSpec((n_dec, d_model), colblk),
            pl.BlockSpec(state2d.shape, const2),
            pl.BlockSpec((1, d_model), const2),
            pl.BlockSpec(w_up_bf.shape, const2, pipeline_mode=pl.Buffered(1)),
            pl.BlockSpec(conv_w.shape, const2),
            pl.BlockSpec(conv_b.shape, const2),
            pl.BlockSpec(w_down_bf.shape, const2, pipeline_mode=pl.Buffered(1)),
            pl.BlockSpec((1, d_model), const2),
        ],
        out_specs=[
            pl.BlockSpec((n_dec, d_model), colblk),
            pl.BlockSpec((n_dec, width), conv_map),
        ],
        out_shape=[
            jax.ShapeDtypeStruct(x2d.shape, F32),
            jax.ShapeDtypeStruct((n_dec, (CONV_W - 1) * width), F32),
        ],
        scratch_shapes=[pltpu.VMEM((CONV_W, n_dec, width), F32)],
        compiler_params=pltpu.CompilerParams(dimension_semantics=("arbitrary",),
                                             vmem_limit_bytes=VMEM_LIMIT),
        name="ffn_sample",
    )(x2d, state2d, g_pre, w_up_bf, conv_w, conv_b, w_down_bf, g_post)


def _rope_tables(pos):
    half = HEAD_DIM // 2
    inv = ROPE_THETA ** (-jnp.arange(half, dtype=F32) * 2.0 / HEAD_DIM)
    ang = pos[:, None] * inv[None, :]
    cos = jnp.cos(ang)
    sin = jnp.sin(ang)
    cos_t = jnp.concatenate([cos, cos, cos, cos], axis=1)
    sin_t = jnp.concatenate([-sin, sin, -sin, sin], axis=1)
    return cos_t, sin_t


def kernel(x_prompt, x_sample, cache_k, cache_v, state_conv, page_table, norm_mix_pre, w_in, lambda_q1,
           lambda_k1, lambda_q2, lambda_k2, subln_g, gate_ln_g, gate_ln_b, w_spatial, b_spatial, w_out,
           norm_mix_post, norm_ffn_pre, w_up, conv_w, conv_b, w_down, norm_ffn_post):
    B, S, d_model = x_prompt.shape
    n_dec, n_t, _ = x_sample.shape
    depth, n_phys, page, n_heads, _, _ = cache_k.shape
    qk_w = n_heads * 2 * HEAD_DIM
    attn_w = n_heads * V_DIM
    gm_w = (w_in.shape[2] - 2 * qk_w - attn_w) // 2
    d_ff = w_down.shape[1]
    past_len = page_table.shape[1] * page
    assert depth == 1 and w_spatial.shape[1] == N_GROUPS and w_spatial.shape[2] == CHUNK
    assert CONV_W - 1 <= n_t <= CHUNK and conv_w.shape[1] == CONV_W

    l = 0
    lam_init = 0.8 - 0.6 * math.exp(-0.3 * l)
    row = lambda a: a[l].reshape(1, -1)
    w_in_bf = w_in[l].astype(BF16)
    w_out_bf = w_out[l].astype(BF16)
    w_up_bf = w_up[l].astype(BF16)
    w_down_bf = w_down[l].astype(BF16)
    lq1, lk1, lq2, lk2 = row(lambda_q1), row(lambda_k1), row(lambda_q2), row(lambda_k2)
    sg = row(subln_g)
    g_mix_pre, g_mix_post = row(norm_mix_pre), row(norm_mix_post)
    g_ffn_pre, g_ffn_post = row(norm_ffn_pre), row(norm_ffn_post)
    ln_g, ln_b = row(gate_ln_g), row(gate_ln_b)
    conv_b2 = row(conv_b)
    gd = gm_w // N_GROUPS

    tm_in = 512
    cos_p, sin_p = _rope_tables(jnp.arange(S, dtype=F32))
    seq_tiles = S // tm_in
    xp2d = x_prompt.reshape(B * S, d_model)
    q_p, kf_p, kb_p, vf_p, vb_p, u_p, vn_p = _mix_in(
        xp2d, g_mix_pre, w_in_bf, cos_p, sin_p, ln_g, ln_b, tm=tm_in,
        row_map=lambda i: (i, 0), tab_map=lambda i: (i % seq_tiles, 0),
        qk_w=qk_w, attn_w=attn_w, gm_w=gm_w, n_rows=B * S, grid=(B * S // tm_in,))
    o_p = _prompt_attn(q_p.reshape(B, S, qk_w), kb_p.reshape(B, S, qk_w), vb_p.reshape(B, S, attn_w),
                       lq1, lk1, lq2, lk2, sg, lam_init=lam_init, n_heads=n_heads, tq=512)
    bs_exp = jnp.repeat(b_spatial[l].T, gd, axis=1)
    x1_p = _mix_out_prompt(xp2d, o_p.reshape(B * S, attn_w), u_p, vn_p, w_spatial[l], bs_exp, w_out_bf,
                           g_mix_post, tm=512)
    y_p, conv_p = _ffn_prompt(x1_p.reshape(B, S, d_model), g_ffn_pre, w_up_bf, conv_w[l], conv_b2, w_down_bf,
                              g_ffn_post, tm=256, col_chunk=256)

    cos_s, sin_s = _rope_tables(jnp.arange(n_t, dtype=F32) + float(past_len))
    cos_s = jnp.repeat(cos_s, n_dec, axis=0)
    sin_s = jnp.repeat(sin_s, n_dec, axis=0)
    xs2d = x_sample.reshape(n_dec, n_t * d_model)
    q_s, kf_s, _, vf_s, _, u_s, vn_s = _mix_in(
        xs2d, g_mix_pre, w_in_bf, cos_s, sin_s, ln_g, ln_b, tm=n_dec,
        row_map=lambda t: (0, t), tab_map=lambda t: (t, 0),
        qk_w=qk_w, attn_w=attn_w, gm_w=gm_w, n_rows=n_dec, grid=(n_t,))
    q_hs = q_s.reshape(n_dec, n_t, n_heads, 2, HEAD_DIM).transpose(0, 3, 2, 1, 4)
    same = ((jnp.arange(2)[:, None, None, None] == jnp.arange(2)[None, None, None, :])
            & (jnp.arange(n_heads)[None, :, None, None] == jnp.arange(n_heads)[None, None, :, None]))
    qbd = jnp.where(same[None, :, :, None, :, :, None], q_hs[:, :, :, :, None, None, :], jnp.zeros((), BF16))
    qbd = qbd.reshape(n_dec, 2 * n_heads * n_t, qk_w)
    o_s = _paged_attn(page_table, qbd, kf_s.reshape(n_dec, n_t, qk_w), vf_s.reshape(n_dec, n_t, attn_w),
                      cache_k[l].transpose(0, 2, 3, 4, 1).reshape(n_phys, qk_w, page), cache_v[l],
                      lq1, lk1, lq2, lk2, sg, lam_init=lam_init, n_heads=n_heads, n_pages_step=8)
    coef = jnp.repeat(w_spatial[l][:, :n_t, :n_t].transpose(1, 2, 0).reshape(n_t * n_t, N_GROUPS), gd, axis=1)
    bias = jnp.repeat(b_spatial[l][:, :n_t].T, gd, axis=1)
    x1_s = _mix_out_sample(xs2d, o_s.reshape(n_dec, n_t * attn_w), u_s, vn_s, coef, bias, w_out_bf, g_mix_post,
                           n_t=n_t)
    y_s, conv_s = _ffn_sample(x1_s, state_conv[l].reshape(n_dec, (CONV_W - 1) * 2 * d_ff), g_ffn_pre, w_up_bf,
                              conv_w[l], conv_b2, w_down_bf, g_ffn_post, n_t=n_t, col_chunk=256)

    return (y_p,
            y_s.reshape(n_dec, n_t, d_model),
            kf_p.reshape(1, B, S, n_heads, 2, HEAD_DIM),
            vf_p.reshape(1, B, S, n_heads, V_DIM),
            conv_p.reshape(1, B, CONV_W - 1, 2 * d_ff),
            kf_s.reshape(1, n_dec, n_t, n_heads, 2, HEAD_DIM),
            vf_s.reshape(1, n_dec, n_t, n_heads, V_DIM),
            conv_s.reshape(1, n_dec, CONV_W - 1, 2 * d_ff),
            vn_s.reshape(1, n_dec, n_t, gm_w))
```

```python
import functools
import math

import jax
import jax.numpy as jnp
from jax import lax
from jax.experimental import pallas as pl
from jax.experimental.pallas import tpu as pltpu

F32 = jnp.float32
BF16 = jnp.bfloat16

HEAD_DIM = 64
V_DIM = 2 * HEAD_DIM
CHUNK = 128
N_GROUPS = 4
CONV_W = 3
ROPE_THETA = 10000.0
NORM_EPS = 1e-6
NEG_INF = -1e30
LANES = 128
SUBLANES = 8
VMEM_LIMIT = 56 * 1024 * 1024

TILE_MIX_IN = 512
TILE_ATTN = 1024
ATTN_Q_SLAB = 256
TILE_MIX_OUT = 512
TILE_FFN = 256
FFN_COL_CHUNK = 256
PAGES_PER_STEP = 16
PAGE_STREAMS = 2


def _rms(x, g):
    return x * lax.rsqrt(jnp.mean(x * x, axis=-1, keepdims=True) + NORM_EPS) * g


def _gelu(x):
    c = math.sqrt(2.0 / math.pi)
    hx = 0.5 * x
    return hx + hx * jnp.tanh(x * (c + (c * 0.044715) * (x * x)))


def _dot(a, b):
    return jnp.dot(a, b, preferred_element_type=F32)


def _diff_lambda(lq1, lk1, lq2, lk2, lam_init):
    a = jnp.sum(lq1 * lk1, axis=-1, keepdims=True)
    b = jnp.sum(lq2 * lk2, axis=-1, keepdims=True)
    return jnp.exp(a) - jnp.exp(b) + lam_init


def _rope(z, cos, sin_signed, lo_half):
    outs = []
    for c in range(z.shape[1] // LANES):
        xs = z[:, c * LANES:(c + 1) * LANES]
        ahead = pltpu.roll(xs, LANES - HEAD_DIM // 2, 1)
        behind = pltpu.roll(xs, HEAD_DIM // 2, 1)
        outs.append(xs * cos + jnp.where(lo_half, ahead, behind) * sin_signed)
    return jnp.concatenate(outs, axis=1)


def _mix_in_core(x, g_ref, w_ref, cos_ref, sin_ref, lng_ref, lnb_ref, *, qk_w, attn_w, gm_w, q_scale):
    h = _rms(x, g_ref[...]).astype(BF16)
    cos = cos_ref[...]
    sin_signed = sin_ref[...]
    lane = lax.broadcasted_iota(jnp.int32, (1, LANES), 1)
    lo_half = (lane % HEAD_DIM) < (HEAD_DIM // 2)
    c0 = 0
    q = _rope(_dot(h, w_ref[:, c0:c0 + qk_w]), cos, sin_signed, lo_half) * q_scale
    c0 += qk_w
    k = _rope(_dot(h, w_ref[:, c0:c0 + qk_w]), cos, sin_signed, lo_half)
    c0 += qk_w
    v = _dot(h, w_ref[:, c0:c0 + attn_w])
    c0 += attn_w
    u = _gelu(_dot(h, w_ref[:, c0:c0 + gm_w]))
    c0 += gm_w
    gv = _gelu(_dot(h, w_ref[:, c0:c0 + gm_w]))
    mu = jnp.mean(gv, axis=-1, keepdims=True)
    gc = gv - mu
    vn = gc * lax.rsqrt(jnp.mean(gc * gc, axis=-1, keepdims=True) + NORM_EPS)
    return q, k, v, u, vn * lng_ref[...] + lnb_ref[...]


def _mix_in_prompt_body(x_ref, g_ref, w_ref, cos_ref, sin_ref, lng_ref, lnb_ref,
                        qt_ref, kt_ref, kb_ref, vf_ref, vt_ref, u_ref, vn_ref, **widths):
    q, k, v, u, vn = _mix_in_core(x_ref[0], g_ref, w_ref, cos_ref, sin_ref, lng_ref, lnb_ref, **widths)
    qt_ref[0] = q.T.astype(BF16)
    kt_ref[0] = k.T
    kb_ref[0] = k.astype(BF16)
    vf_ref[0] = v
    vt_ref[0] = v.T.astype(BF16)
    u_ref[0] = u
    vn_ref[0] = vn


def _mix_in_prompt(x3d, g_pre, w_in_bf, cos_t, sin_t, ln_g, ln_b, *, tm, qk_w, attn_w, gm_w):
    B, S, d_model = x3d.shape
    const = lambda b, j: (0, 0)
    rows = lambda b, j: (b, j, 0)
    cols = lambda b, j: (b, 0, j)

    def natural(width, dtype):
        return jax.ShapeDtypeStruct((B, S, width), dtype), pl.BlockSpec((1, tm, width), rows)

    def transposed(width, dtype):
        return jax.ShapeDtypeStruct((B, width, S), dtype), pl.BlockSpec((1, width, tm), cols)

    outs = [transposed(qk_w, BF16), transposed(qk_w, F32), natural(qk_w, BF16), natural(attn_w, F32),
            transposed(attn_w, BF16), natural(gm_w, F32), natural(gm_w, F32)]
    return pl.pallas_call(
        functools.partial(_mix_in_prompt_body, qk_w=qk_w, attn_w=attn_w, gm_w=gm_w,
                          q_scale=HEAD_DIM ** -0.5 * math.log2(math.e)),
        grid=(B, S // tm),
        in_specs=[
            pl.BlockSpec((1, tm, d_model), rows),
            pl.BlockSpec((1, d_model), const),
            pl.BlockSpec(w_in_bf.shape, const),
            pl.BlockSpec((tm, LANES), lambda b, j: (j, 0)),
            pl.BlockSpec((tm, LANES), lambda b, j: (j, 0)),
            pl.BlockSpec((1, gm_w), const),
            pl.BlockSpec((1, gm_w), const),
        ],
        out_specs=[o[1] for o in outs],
        out_shape=[o[0] for o in outs],
        compiler_params=pltpu.CompilerParams(dimension_semantics=("parallel", "parallel"),
                                             vmem_limit_bytes=VMEM_LIMIT),
        name="mix_in_prompt",
    )(x3d, g_pre, w_in_bf, cos_t, sin_t, ln_g, ln_b)


def _mix_in_sample_body(x_ref, g_ref, w_ref, cos_ref, sin_ref, lng_ref, lnb_ref,
                        q_ref, k_ref, v_ref, u_ref, vn_ref, **widths):
    q, k, v, u, vn = _mix_in_core(x_ref[...], g_ref, w_ref, cos_ref, sin_ref, lng_ref, lnb_ref, **widths)
    q_ref[...] = q.astype(BF16)
    k_ref[...] = k
    v_ref[...] = v
    u_ref[...] = u
    vn_ref[...] = vn


def _mix_in_sample(x2d, g_pre, w_in_bf, cos_t, sin_t, ln_g, ln_b, *, n_t, qk_w, attn_w, gm_w):
    n_dec = x2d.shape[0]
    d_model = w_in_bf.shape[0]
    const = lambda t: (0, 0)
    colblk = lambda t: (0, t)

    def out(width, dtype):
        return jax.ShapeDtypeStruct((n_dec, n_t * width), dtype), pl.BlockSpec((n_dec, width), colblk)

    outs = [out(qk_w, BF16), out(qk_w, F32), out(attn_w, F32), out(gm_w, F32), out(gm_w, F32)]
    return pl.pallas_call(
        functools.partial(_mix_in_sample_body, qk_w=qk_w, attn_w=attn_w, gm_w=gm_w, q_scale=HEAD_DIM ** -0.5),
        grid=(n_t,),
        in_specs=[
            pl.BlockSpec((n_dec, d_model), colblk),
            pl.BlockSpec((1, d_model), const),
            pl.BlockSpec(w_in_bf.shape, const),
            pl.BlockSpec((n_dec, LANES), lambda t: (t, 0)),
            pl.BlockSpec((n_dec, LANES), lambda t: (t, 0)),
            pl.BlockSpec((1, gm_w), const),
            pl.BlockSpec((1, gm_w), const),
        ],
        out_specs=[o[1] for o in outs],
        out_shape=[o[0] for o in outs],
        compiler_params=pltpu.CompilerParams(dimension_semantics=("parallel",),
                                             vmem_limit_bytes=VMEM_LIMIT),
        name="mix_in_sample",
    )(x2d, g_pre, w_in_bf, cos_t, sin_t, ln_g, ln_b)


def _prompt_attn_body(qi_ref, kj_ref, qt_ref, k_ref, vt_ref, lq1_ref, lk1_ref, lq2_ref, lk2_ref, sg_ref,
                      o_ref, m_sc, l_sc, a_sc, *, lam_init, q_slab):
    p = pl.program_id(2)
    qi = qi_ref[p]
    kj = kj_ref[p]
    tq = qt_ref.shape[2]
    tk = k_ref.shape[1]
    units = [(s, c) for s in range(2) for c in range(tq // q_slab)]

    @pl.when(kj == 0)
    def _():
        m_sc[...] = jnp.full(m_sc.shape, NEG_INF, F32)
        l_sc[...] = jnp.zeros(l_sc.shape, F32)
        a_sc[...] = jnp.zeros(a_sc.shape, F32)

    def block(diagonal):
        n_keys = (lambda c: (c + 1) * q_slab) if diagonal else (lambda c: tk)

        def scores(s, c):
            qs = qt_ref[0, :, c * q_slab:(c + 1) * q_slab]
            zeros = jnp.zeros((HEAD_DIM, q_slab), BF16)
            q_half = (jnp.concatenate([qs[:HEAD_DIM], zeros], axis=0) if s == 0
                      else jnp.concatenate([zeros, qs[HEAD_DIM:]], axis=0))
            return _dot(k_ref[0, :n_keys(c), :], q_half)

        def accumulate(st, s, c):
            cols = slice(c * q_slab, (c + 1) * q_slab)
            if diagonal:
                key = lax.broadcasted_iota(jnp.int32, st.shape, 0)
                qry = lax.broadcasted_iota(jnp.int32, st.shape, 1) + c * q_slab
                st = jnp.where(key <= qry, st, NEG_INF)
            m_prev = m_sc[s, :, cols]
            m_new = jnp.maximum(m_prev, jnp.max(st, axis=0, keepdims=True))
            alpha = jnp.exp2(m_prev - m_new)
            pt = jnp.exp2(st - m_new)
            l_sc[s, :, cols] = alpha * l_sc[s, :, cols] + jnp.sum(pt, axis=0, keepdims=True)
            a_sc[s, :, cols] = alpha * a_sc[s, :, cols] + _dot(vt_ref[0, :, :n_keys(c)], pt.astype(BF16))
            m_sc[s, :, cols] = m_new

        st_next = scores(*units[0])
        for u, (s, c) in enumerate(units):
            st = st_next
            if u + 1 < len(units):
                st_next = scores(*units[u + 1])
            accumulate(st, s, c)

    @pl.when(kj < qi)
    def _():
        block(False)

    @pl.when(kj == qi)
    def _():
        block(True)
        lam = _diff_lambda(lq1_ref[...], lk1_ref[...], lq2_ref[...], lk2_ref[...], lam_init)
        ot = a_sc[0] / l_sc[0] - lam * (a_sc[1] / l_sc[1])
        ms = jnp.mean(ot * ot, axis=0, keepdims=True)
        y = ot * lax.rsqrt(ms + NORM_EPS) * sg_ref[...] * (1.0 - lam_init)
        o_ref[0] = y.T


def _prompt_attn(qt, k, vt, lq1, lk1, lq2, lk2, subln_g_col, *, lam_init, n_heads, tq):
    B, S, _ = k.shape
    nq = S // tq
    pairs = [(i, j) for i in range(nq) for j in range(i + 1)]
    qi = jnp.asarray([p[0] for p in pairs], jnp.int32)
    kj = jnp.asarray([p[1] for p in pairs], jnp.int32)
    const = lambda b, h, p, qi, kj: (0, 0)
    grid_spec = pltpu.PrefetchScalarGridSpec(
        num_scalar_prefetch=2,
        grid=(B, n_heads, len(pairs)),
        in_specs=[
            pl.BlockSpec((1, V_DIM, tq), lambda b, h, p, qi, kj: (b, h, qi[p])),
            pl.BlockSpec((1, tq, V_DIM), lambda b, h, p, qi, kj: (b, kj[p], h)),
            pl.BlockSpec((1, V_DIM, tq), lambda b, h, p, qi, kj: (b, h, kj[p])),
            pl.BlockSpec((1, HEAD_DIM), const),
            pl.BlockSpec((1, HEAD_DIM), const),
            pl.BlockSpec((1, HEAD_DIM), const),
            pl.BlockSpec((1, HEAD_DIM), const),
            pl.BlockSpec((V_DIM, 1), const),
        ],
        out_specs=pl.BlockSpec((1, tq, V_DIM), lambda b, h, p, qi, kj: (b, qi[p], h)),
        scratch_shapes=[pltpu.VMEM((2, 1, tq), F32), pltpu.VMEM((2, 1, tq), F32), pltpu.VMEM((2, V_DIM, tq), F32)],
    )
    return pl.pallas_call(
        functools.partial(_prompt_attn_body, lam_init=lam_init, q_slab=ATTN_Q_SLAB),
        grid_spec=grid_spec,
        out_shape=jax.ShapeDtypeStruct((B, S, n_heads * V_DIM), F32),
        compiler_params=pltpu.CompilerParams(
            dimension_semantics=("parallel", "parallel", "arbitrary"), vmem_limit_bytes=VMEM_LIMIT),
        name="prompt_attn",
    )(qi, kj, qt, k, vt, lq1, lk1, lq2, lk2, subln_g_col)


def _paged_attn_body(pt_ref, qbd_ref, kn_ref, vn_ref, lq1_ref, lk1_ref, lq2_ref, lk2_ref, sg_ref, *rest,
                     lam_init, n_pages_step, n_streams, n_heads, n_new):
    k_refs = rest[:n_pages_step]
    v_refs = rest[n_pages_step:2 * n_pages_step]
    o_ref = rest[2 * n_pages_step]
    m_sc, l_sc, acc_sc, y_sc = rest[2 * n_pages_step + 1:]
    g = pl.program_id(1)
    n_rows = qbd_ref.shape[1]
    half = n_rows // 2
    page = k_refs[0].shape[2]
    per = n_pages_step // n_streams

    @pl.when(g == 0)
    def _():
        m_sc[...] = jnp.full(m_sc.shape, NEG_INF, F32)
        l_sc[...] = jnp.zeros(l_sc.shape, F32)
        acc_sc[...] = jnp.zeros(acc_sc.shape, F32)

    qbd = qbd_ref[0]

    def scores(t):
        return jnp.concatenate([_dot(qbd, r[0].astype(BF16)) for r in k_refs[t * per:(t + 1) * per]], axis=1)

    def accumulate(s, t):
        m_prev = m_sc[t]
        m_new = jnp.maximum(m_prev, jnp.max(s, axis=-1, keepdims=True))
        alpha = jnp.exp(m_prev - m_new)
        pr = jnp.exp(s - m_new)
        l_sc[t] = alpha * l_sc[t] + jnp.sum(pr, axis=-1, keepdims=True)
        prb = pr.astype(BF16)
        for h in range(n_heads):
            cols = slice(h * V_DIM, (h + 1) * V_DIM)
            v_h = jnp.concatenate([r[pl.ds(0, 1), pl.ds(h, page, stride=n_heads), :][0]
                                   for r in v_refs[t * per:(t + 1) * per]], axis=0).astype(BF16)
            acc_sc[t, :, cols] = alpha * acc_sc[t, :, cols] + _dot(prb, v_h)
        m_sc[t] = m_new

    s_next = scores(0)
    for t in range(n_streams):
        s = s_next
        if t + 1 < n_streams:
            s_next = scores(t + 1)
        accumulate(s, t)

    @pl.when(g == pl.num_programs(1) - 1)
    def _():
        qf = qbd.astype(F32)
        q_tok = lax.broadcasted_iota(jnp.int32, (n_rows, 1), 0) % n_new
        s_new = []
        for j in range(n_new):
            kj = kn_ref[0, j:j + 1, :].astype(BF16).astype(F32)
            s_new.append(jnp.where(q_tok >= j, jnp.sum(qf * kj, axis=-1, keepdims=True), NEG_INF))
        m_fin = m_sc[0]
        for t in range(1, n_streams):
            m_fin = jnp.maximum(m_fin, m_sc[t])
        for sj in s_new:
            m_fin = jnp.maximum(m_fin, sj)
        l_fin = jnp.zeros_like(m_fin)
        acc = jnp.zeros(acc_sc.shape[1:], F32)
        for t in range(n_streams):
            alpha = jnp.exp(m_sc[t] - m_fin)
            l_fin = l_fin + alpha * l_sc[t]
            acc = acc + alpha * acc_sc[t]
        for j, sj in enumerate(s_new):
            pj = jnp.exp(sj - m_fin)
            l_fin = l_fin + pj
            acc = acc + pj.astype(BF16).astype(F32) * vn_ref[0, j:j + 1, :].astype(BF16).astype(F32)
        lam = _diff_lambda(lq1_ref[...], lk1_ref[...], lq2_ref[...], lk2_ref[...], lam_init)
        o = acc / l_fin
        o = o[:half] - lam * o[half:]
        for h in range(n_heads):
            blk = o[:, h * V_DIM:(h + 1) * V_DIM]
            y_sc[:, h * V_DIM:(h + 1) * V_DIM] = _rms(blk, sg_ref[...]) * (1.0 - lam_init)
        for h in range(n_heads):
            o_ref[0, :, h * V_DIM:(h + 1) * V_DIM] = y_sc[h * n_new:(h + 1) * n_new, h * V_DIM:(h + 1) * V_DIM]


def _paged_attn(page_table, qbd, k_new, v_new, cache_kt, cache_v, lq1, lk1, lq2, lk2, subln_g, *,
                lam_init, n_heads, n_pages_step):
    n_dec, n_pages = page_table.shape
    _, width, page = cache_kt.shape
    n_new = k_new.shape[1]
    n_rows = qbd.shape[1]
    G = n_pages_step
    const = lambda b, g, pt: (0, 0)
    per_b = lambda b, g, pt: (b, 0, 0)

    def k_spec(j):
        return pl.BlockSpec((1, width, page), lambda b, g, pt: (pt[b, g * G + j], 0, 0))

    def v_spec(j):
        return pl.BlockSpec((1, page * n_heads, V_DIM), lambda b, g, pt: (pt[b, g * G + j], 0, 0))

    grid_spec = pltpu.PrefetchScalarGridSpec(
        num_scalar_prefetch=1,
        grid=(n_dec, n_pages // G),
        in_specs=[
            pl.BlockSpec((1, n_rows, width), per_b),
            pl.BlockSpec((1, n_new, width), per_b),
            pl.BlockSpec((1, n_new, width), per_b),
            pl.BlockSpec((1, HEAD_DIM), const),
            pl.BlockSpec((1, HEAD_DIM), const),
            pl.BlockSpec((1, HEAD_DIM), const),
            pl.BlockSpec((1, HEAD_DIM), const),
            pl.BlockSpec((1, V_DIM), const),
        ] + [k_spec(j) for j in range(G)] + [v_spec(j) for j in range(G)],
        out_specs=pl.BlockSpec((1, n_new, width), per_b),
        scratch_shapes=[
            pltpu.VMEM((PAGE_STREAMS, n_rows, 1), F32), pltpu.VMEM((PAGE_STREAMS, n_rows, 1), F32),
            pltpu.VMEM((PAGE_STREAMS, n_rows, width), F32), pltpu.VMEM((n_rows // 2, width), F32),
        ],
    )
    return pl.pallas_call(
        functools.partial(_paged_attn_body, lam_init=lam_init, n_pages_step=G, n_streams=PAGE_STREAMS,
                          n_heads=n_heads, n_new=n_new),
        grid_spec=grid_spec,
        out_shape=jax.ShapeDtypeStruct((n_dec, n_new, width), F32),
        compiler_params=pltpu.CompilerParams(
            dimension_semantics=("parallel", "arbitrary"), vmem_limit_bytes=VMEM_LIMIT),
        name="paged_attn",
    )(page_table, qbd, k_new, v_new, lq1, lk1, lq2, lk2, subln_g, *([cache_kt] * G), *([cache_v] * G))


def _mix_out_prompt_body(x_ref, o_ref, u_ref, vn_ref, ws_ref, bs_ref, wout_ref, gpost_ref, y_ref, gate_sc):
    tm = x_ref.shape[0]
    attn_w = o_ref.shape[1]
    gd = vn_ref.shape[1] // N_GROUPS
    row = lax.broadcasted_iota(jnp.int32, (CHUNK, CHUNK), 0)
    col = lax.broadcasted_iota(jnp.int32, (CHUNK, CHUNK), 1)
    for g in range(N_GROUPS):
        wg = jnp.where(col <= row, ws_ref[g], 0.0).astype(BF16)
        bias = bs_ref[:, g * gd:(g + 1) * gd]
        for c in range(tm // CHUNK):
            rows = slice(c * CHUNK, (c + 1) * CHUNK)
            cols = slice(g * gd, (g + 1) * gd)
            s = _dot(wg, vn_ref[rows, cols].astype(BF16)) + bias
            gate_sc[rows, cols] = (u_ref[rows, cols] * s).astype(BF16)
    m = _dot(o_ref[...].astype(BF16), wout_ref[:attn_w, :]) + _dot(gate_sc[...], wout_ref[attn_w:, :])
    y_ref[...] = x_ref[...] + _rms(m, gpost_ref[...])


def _mix_out_prompt(x2d, o2d, u2d, vn2d, w_spatial, bs_exp, w_out_bf, g_post, *, tm):
    n_rows, d_model = x2d.shape
    attn_w = o2d.shape[1]
    gm_w = u2d.shape[1]
    const2 = lambda i: (0, 0)
    rows = lambda i: (i, 0)
    return pl.pallas_call(
        _mix_out_prompt_body,
        grid=(n_rows // tm,),
        in_specs=[
            pl.BlockSpec((tm, d_model), rows),
            pl.BlockSpec((tm, attn_w), rows),
            pl.BlockSpec((tm, gm_w), rows),
            pl.BlockSpec((tm, gm_w), rows),
            pl.BlockSpec(w_spatial.shape, lambda i: (0, 0, 0)),
            pl.BlockSpec(bs_exp.shape, const2),
            pl.BlockSpec(w_out_bf.shape, const2),
            pl.BlockSpec((1, d_model), const2),
        ],
        out_specs=pl.BlockSpec((tm, d_model), rows),
        out_shape=jax.ShapeDtypeStruct((n_rows, d_model), F32),
        scratch_shapes=[pltpu.VMEM((tm, gm_w), BF16)],
        compiler_params=pltpu.CompilerParams(dimension_semantics=("parallel",),
                                             vmem_limit_bytes=VMEM_LIMIT),
        name="mix_out_prompt",
    )(x2d, o2d, u2d, vn2d, w_spatial, bs_exp, w_out_bf, g_post)


def _mix_out_sample_body(x_ref, o_ref, u_ref, vn_ref, coef_ref, bias_ref, wout_ref, gpost_ref, y_ref, hist):
    t = pl.program_id(0)
    n_t = hist.shape[0]
    attn_w = o_ref.shape[1]

    @pl.when(t == 0)
    def _():
        hist[...] = jnp.zeros(hist.shape, F32)

    hist[t] = vn_ref[...]
    s = bias_ref[pl.ds(t, 1), :]
    for j in range(n_t):
        cj = jnp.where(j <= t, coef_ref[pl.ds(t * n_t + j, 1), :], 0.0)
        s = s + cj.astype(BF16).astype(F32) * hist[j].astype(BF16).astype(F32)
    gate = (u_ref[...] * s).astype(BF16)
    m = _dot(o_ref[...].astype(BF16), wout_ref[:attn_w, :]) + _dot(gate, wout_ref[attn_w:, :])
    y_ref[...] = x_ref[...] + _rms(m, gpost_ref[...])


def _mix_out_sample(x2d, o2d, u2d, vn2d, coef, bias, w_out_bf, g_post, *, n_t):
    n_dec = x2d.shape[0]
    d_model = x2d.shape[1] // n_t
    attn_w = o2d.shape[1] // n_t
    gm_w = u2d.shape[1] // n_t
    const2 = lambda t: (0, 0)
    colblk = lambda t: (0, t)
    return pl.pallas_call(
        _mix_out_sample_body,
        grid=(n_t,),
        in_specs=[
            pl.BlockSpec((n_dec, d_model), colblk),
            pl.BlockSpec((n_dec, attn_w), colblk),
            pl.BlockSpec((n_dec, gm_w), colblk),
            pl.BlockSpec((n_dec, gm_w), colblk),
            pl.BlockSpec(coef.shape, const2),
            pl.BlockSpec(bias.shape, const2),
            pl.BlockSpec(w_out_bf.shape, const2),
            pl.BlockSpec((1, d_model), const2),
        ],
        out_specs=pl.BlockSpec((n_dec, d_model), colblk),
        out_shape=jax.ShapeDtypeStruct(x2d.shape, F32),
        scratch_shapes=[pltpu.VMEM((n_t, n_dec, gm_w), F32)],
        compiler_params=pltpu.CompilerParams(dimension_semantics=("arbitrary",),
                                             vmem_limit_bytes=VMEM_LIMIT),
        name="mix_out_sample",
    )(x2d, o2d, u2d, vn2d, coef, bias, w_out_bf, g_post)


def _ffn_chunks(h, taps, cw_ref, cb_ref, wup_ref, wdn_ref, *, d_ff, col_chunk):
    n_chunks = d_ff // col_chunk
    gate_cols = lambda c: slice(c * col_chunk, (c + 1) * col_chunk)
    val_cols = lambda c: slice(d_ff + c * col_chunk, d_ff + (c + 1) * col_chunk)

    def up(c):
        return _dot(h, wup_ref[:, gate_cols(c)]), _dot(h, wup_ref[:, val_cols(c)])

    def conv(u, cols):
        shifted = taps(u, cols)
        y = cb_ref[:, cols] + cw_ref[0:1, cols] * shifted[0]
        for j in range(1, CONV_W):
            y = y + cw_ref[j:j + 1, cols] * shifted[j]
        return y

    def down(a, c):
        return _dot(a, wdn_ref[gate_cols(c), :])

    acc = None
    act = None
    up_next = up(0)
    for c in range(n_chunks):
        up_gate, up_val = up_next
        if c + 1 < n_chunks:
            up_next = up(c + 1)
        if act is not None:
            part = down(act, c - 1)
            acc = part if acc is None else acc + part
        act = (_gelu(conv(up_gate, gate_cols(c))) * conv(up_val, val_cols(c))).astype(BF16)
    part = down(act, n_chunks - 1)
    return part if acc is None else acc + part


def _ffn_prompt_body(x_ref, g_ref, wup_ref, cw_ref, cb_ref, wdn_ref, gpost_ref, y_ref, conv_ref, carry1, carry2, *,
                     d_ff, col_chunk):
    j = pl.program_id(1)

    @pl.when(j == 0)
    def _():
        carry1[...] = jnp.zeros(carry1.shape, F32)
        carry2[...] = jnp.zeros(carry2.shape, F32)

    x = x_ref[0]
    h = _rms(x, g_ref[...]).astype(BF16)
    row = lax.broadcasted_iota(jnp.int32, (SUBLANES, 1), 0)

    def taps(up, cols):
        r1 = pltpu.roll(up, 1, 0)
        r2 = pltpu.roll(up, 2, 0)
        t1 = jnp.concatenate([jnp.where(row < 1, carry1[:, cols], r1[:SUBLANES]), r1[SUBLANES:]], axis=0)
        t2 = jnp.concatenate([jnp.where(row < 2, carry2[:, cols], r2[:SUBLANES]), r2[SUBLANES:]], axis=0)
        carry1[:, cols] = r1[:SUBLANES]
        carry2[:, cols] = r2[:SUBLANES]
        conv_ref[0, :, cols] = r2[:CONV_W - 1]
        return [t2, t1, up]

    acc = _ffn_chunks(h, taps, cw_ref, cb_ref, wup_ref, wdn_ref, d_ff=d_ff, col_chunk=col_chunk)
    y_ref[0] = x + _rms(acc, gpost_ref[...])


def _ffn_prompt(x3d, g_pre, w_up_bf, conv_w, conv_b, w_down_bf, g_post, *, tm, col_chunk):
    B, S, d_model = x3d.shape
    d_ff = w_down_bf.shape[0]
    const2 = lambda b, j: (0, 0)
    tile = lambda b, j: (b, j, 0)
    return pl.pallas_call(
        functools.partial(_ffn_prompt_body, d_ff=d_ff, col_chunk=col_chunk),
        grid=(B, S // tm),
        in_specs=[
            pl.BlockSpec((1, tm, d_model), tile),
            pl.BlockSpec((1, d_model), const2),
            pl.BlockSpec(w_up_bf.shape, const2, pipeline_mode=pl.Buffered(1)),
            pl.BlockSpec(conv_w.shape, const2),
            pl.BlockSpec(conv_b.shape, const2),
            pl.BlockSpec(w_down_bf.shape, const2, pipeline_mode=pl.Buffered(1)),
            pl.BlockSpec((1, d_model), const2),
        ],
        out_specs=[
            pl.BlockSpec((1, tm, d_model), tile),
            pl.BlockSpec((1, CONV_W - 1, 2 * d_ff), lambda b, j: (b, 0, 0)),
        ],
        out_shape=[
            jax.ShapeDtypeStruct((B, S, d_model), F32),
            jax.ShapeDtypeStruct((B, CONV_W - 1, 2 * d_ff), F32),
        ],
        scratch_shapes=[pltpu.VMEM((SUBLANES, 2 * d_ff), F32), pltpu.VMEM((SUBLANES, 2 * d_ff), F32)],
        compiler_params=pltpu.CompilerParams(dimension_semantics=("parallel", "arbitrary"),
                                             vmem_limit_bytes=VMEM_LIMIT),
        name="ffn_prompt",
    )(x3d, g_pre, w_up_bf, conv_w, conv_b, w_down_bf, g_post)


def _ffn_sample_body(x_ref, st_ref, g_ref, wup_ref, cw_ref, cb_ref, wdn_ref, gpost_ref, y_ref, conv_ref, ring, *,
                     d_ff, col_chunk):
    t = pl.program_id(0)
    width = 2 * d_ff

    @pl.when(t == 0)
    def _():
        for j in range(CONV_W - 1):
            ring[j] = st_ref[:, j * width:(j + 1) * width]

    x = x_ref[...]
    h = _rms(x, g_ref[...]).astype(BF16)
    cur = (t + CONV_W - 1) % CONV_W

    def taps(up, cols):
        ring[cur, :, cols] = up
        conv_ref[:, cols] = up
        return [ring[(t + j) % CONV_W, :, cols] for j in range(CONV_W - 1)] + [up]

    acc = _ffn_chunks(h, taps, cw_ref, cb_ref, wup_ref, wdn_ref, d_ff=d_ff, col_chunk=col_chunk)
    y_ref[...] = x + _rms(acc, gpost_ref[...])


def _ffn_sample(x2d, state2d, g_pre, w_up_bf, conv_w, conv_b, w_down_bf, g_post, *, n_t, col_chunk):
    n_dec = x2d.shape[0]
    d_model = x2d.shape[1] // n_t
    d_ff = w_down_bf.shape[0]
    width = 2 * d_ff
    const2 = lambda t: (0, 0)
    colblk = lambda t: (0, t)
    conv_map = lambda t: (0, jnp.maximum(t - (n_t - (CONV_W - 1)), 0))
    return pl.pallas_call(
        functools.partial(_ffn_sample_body, d_ff=d_ff, col_chunk=col_chunk),
        grid=(n_t,),
        in_specs=[
            pl.BlockSpec((n_dec, d_model), colblk),
            pl.BlockSpec(state2d.shape, const2),
            pl.BlockSpec((1, d_model), const2),
            pl.BlockSpec(w_up_bf.shape, const2, pipeline_mode=pl.Buffered(1)),
            pl.BlockSpec(conv_w.shape, const2),
            pl.BlockSpec(conv_b.shape, const2),
            pl.BlockSpec(w_down_bf.shape, const2, pipeline_mode=pl.Buffered(1)),
            pl.BlockSpec((1, d_model), const2),
        ],
        out_specs=[
            pl.BlockSpec((n_dec, d_model), colblk),
            pl.BlockSpec((n_dec, width), conv_map),
        ],
        out_shape=[
            jax.ShapeDtypeStruct(x2d.shape, F32),
            jax.ShapeDtypeStruct((n_dec, (CONV_W - 1) * width), F32),
        ],
        scratch_shapes=[pltpu.VMEM((CONV_W, n_dec, width), F32)],
        compiler_params=pltpu.CompilerParams(dimension_semantics=("arbitrary",),
                                             vmem_limit_bytes=VMEM_LIMIT),
        name="ffn_sample",
    )(x2d, state2d, g_pre, w_up_bf, conv_w, conv_b, w_down_bf, g_post)


def _rope_tables(pos):
    half = HEAD_DIM // 2
    inv = ROPE_THETA ** (-jnp.arange(half, dtype=F32) * 2.0 / HEAD_DIM)
    ang = pos[:, None] * inv[None, :]
    cos = jnp.cos(ang)
    sin = jnp.sin(ang)
    cos_t = jnp.concatenate([cos, cos, cos, cos], axis=1)
    sin_t = jnp.concatenate([-sin, sin, -sin, sin], axis=1)
    return cos_t, sin_t


def kernel(x_prompt, x_sample, cache_k, cache_v, state_conv, page_table, norm_mix_pre, w_in, lambda_q1,
           lambda_k1, lambda_q2, lambda_k2, subln_g, gate_ln_g, gate_ln_b, w_spatial, b_spatial, w_out,
           norm_mix_post, norm_ffn_pre, w_up, conv_w, conv_b, w_down, norm_ffn_post):
    B, S, d_model = x_prompt.shape
    n_dec, n_t, _ = x_sample.shape
    depth, n_phys, page, n_heads, _, _ = cache_k.shape
    qk_w = n_heads * 2 * HEAD_DIM
    attn_w = n_heads * V_DIM
    gm_w = (w_in.shape[2] - 2 * qk_w - attn_w) // 2
    d_ff = w_down.shape[1]
    past_len = page_table.shape[1] * page
    assert depth == 1 and w_spatial.shape[1] == N_GROUPS and w_spatial.shape[2] == CHUNK
    assert CONV_W - 1 <= n_t <= CHUNK and conv_w.shape[1] == CONV_W
    widths = dict(qk_w=qk_w, attn_w=attn_w, gm_w=gm_w)

    l = 0
    lam_init = 0.8 - 0.6 * math.exp(-0.3 * l)
    row = lambda a: a[l].reshape(1, -1)
    w_in_bf = w_in[l].astype(BF16)
    w_out_bf = w_out[l].astype(BF16)
    w_up_bf = w_up[l].astype(BF16)
    w_down_bf = w_down[l].astype(BF16)
    lq1, lk1, lq2, lk2 = row(lambda_q1), row(lambda_k1), row(lambda_q2), row(lambda_k2)
    sg = row(subln_g)
    g_mix_pre, g_mix_post = row(norm_mix_pre), row(norm_mix_post)
    g_ffn_pre, g_ffn_post = row(norm_ffn_pre), row(norm_ffn_post)
    ln_g, ln_b = row(gate_ln_g), row(gate_ln_b)
    conv_b2 = row(conv_b)
    gd = gm_w // N_GROUPS

    cos_p, sin_p = _rope_tables(jnp.arange(S, dtype=F32))
    qt_p, kt_p, kb_p, vf_p, vt_p, u_p, vn_p = _mix_in_prompt(
        x_prompt, g_mix_pre, w_in_bf, cos_p, sin_p, ln_g, ln_b, tm=TILE_MIX_IN, **widths)
    o_p = _prompt_attn(qt_p, kb_p, vt_p, lq1, lk1, lq2, lk2, sg.reshape(V_DIM, 1),
                       lam_init=lam_init, n_heads=n_heads, tq=TILE_ATTN)
    bs_exp = jnp.repeat(b_spatial[l].T, gd, axis=1)
    x1_p = _mix_out_prompt(x_prompt.reshape(B * S, d_model), o_p.reshape(B * S, attn_w),
                           u_p.reshape(B * S, gm_w), vn_p.reshape(B * S, gm_w), w_spatial[l], bs_exp, w_out_bf,
                           g_mix_post, tm=TILE_MIX_OUT)
    y_p, conv_p = _ffn_prompt(x1_p.reshape(B, S, d_model), g_ffn_pre, w_up_bf, conv_w[l], conv_b2, w_down_bf,
                              g_ffn_post, tm=TILE_FFN, col_chunk=FFN_COL_CHUNK)
    new_k_p = kt_p.reshape(B, n_heads, 2, HEAD_DIM, S).transpose(0, 4, 1, 2, 3)[None]

    cos_s, sin_s = _rope_tables(jnp.arange(n_t, dtype=F32) + float(past_len))
    cos_s = jnp.repeat(cos_s, n_dec, axis=0)
    sin_s = jnp.repeat(sin_s, n_dec, axis=0)
    xs2d = x_sample.reshape(n_dec, n_t * d_model)
    q_s, kf_s, vf_s, u_s, vn_s = _mix_in_sample(xs2d, g_mix_pre, w_in_bf, cos_s, sin_s, ln_g, ln_b, n_t=n_t,
                                                **widths)
    q_hs = q_s.reshape(n_dec, n_t, n_heads, 2, HEAD_DIM).transpose(0, 3, 2, 1, 4)
    same = ((jnp.arange(2)[:, None, None, None] == jnp.arange(2)[None, None, None, :])
            & (jnp.arange(n_heads)[None, :, None, None] == jnp.arange(n_heads)[None, None, :, None]))
    qbd = jnp.where(same[None, :, :, None, :, :, None], q_hs[:, :, :, :, None, None, :], jnp.zeros((), BF16))
    qbd = qbd.reshape(n_dec, 2 * n_heads * n_t, qk_w)
    cache_kt = cache_k[l].transpose(0, 2, 3, 4, 1).reshape(n_phys, qk_w, page)
    cache_v2 = cache_v[l].reshape(n_phys, page * n_heads, V_DIM)
    o_s = _paged_attn(page_table, qbd, kf_s.reshape(n_dec, n_t, qk_w), vf_s.reshape(n_dec, n_t, attn_w),
                      cache_kt, cache_v2, lq1, lk1, lq2, lk2, sg, lam_init=lam_init, n_heads=n_heads,
                      n_pages_step=PAGES_PER_STEP)
    coef = jnp.repeat(w_spatial[l][:, :n_t, :n_t].transpose(1, 2, 0).reshape(n_t * n_t, N_GROUPS), gd, axis=1)
    bias = jnp.repeat(b_spatial[l][:, :n_t].T, gd, axis=1)
    x1_s = _mix_out_sample(xs2d, o_s.reshape(n_dec, n_t * attn_w), u_s, vn_s, coef, bias, w_out_bf, g_mix_post,
                           n_t=n_t)
    y_s, conv_s = _ffn_sample(x1_s, state_conv[l].reshape(n_dec, (CONV_W - 1) * 2 * d_ff), g_ffn_pre, w_up_bf,
                              conv_w[l], conv_b2, w_down_bf, g_ffn_post, n_t=n_t, col_chunk=FFN_COL_CHUNK)

    return (y_p,
            y_s.reshape(n_dec, n_t, d_model),
            new_k_p,
            vf_p.reshape(1, B, S, n_heads, V_DIM),
            conv_p.reshape(1, B, CONV_W - 1, 2 * d_ff),
            kf_s.reshape(1, n_dec, n_t, n_heads, 2, HEAD_DIM),
            vf_s.reshape(1, n_dec, n_t, n_heads, V_DIM),
            conv_s.reshape(1, n_dec, CONV_W - 1, 2 * d_ff),
            vn_s.reshape(1, n_dec, n_t, gm_w))
```

```python
import functools
import math

import jax
import jax.numpy as jnp
from jax import lax
from jax.experimental import pallas as pl
from jax.experimental.pallas import tpu as pltpu

F32 = jnp.float32
BF16 = jnp.bfloat16

HEAD_DIM = 64
V_DIM = 2 * HEAD_DIM
CHUNK = 128
N_GROUPS = 4
CONV_W = 3
ROPE_THETA = 10000.0
NORM_EPS = 1e-6
NEG_INF = -1e30
LANES = 128
SUBLANES = 8
VMEM_LIMIT = 56 * 1024 * 1024

TILE_MIX_IN = 512
TILE_ATTN = 2048
ATTN_Q_SLAB = 256
TILE_MIX_OUT = 512
TILE_FFN = 256
FFN_COL_CHUNK = 256
PAGES_PER_STEP = 16
PAGE_STREAMS = 2


def _rms(x, g):
    return x * lax.rsqrt(jnp.mean(x * x, axis=-1, keepdims=True) + NORM_EPS) * g


def _gelu(x):
    c = math.sqrt(2.0 / math.pi)
    hx = 0.5 * x
    return hx + hx * jnp.tanh(x * (c + (c * 0.044715) * (x * x)))


def _dot(a, b):
    return jnp.dot(a, b, preferred_element_type=F32)


def _diff_lambda(lq1, lk1, lq2, lk2, lam_init):
    a = jnp.sum(lq1 * lk1, axis=-1, keepdims=True)
    b = jnp.sum(lq2 * lk2, axis=-1, keepdims=True)
    return jnp.exp(a) - jnp.exp(b) + lam_init


def _rope(z, cos, sin_signed, lo_half):
    outs = []
    for c in range(z.shape[1] // LANES):
        xs = z[:, c * LANES:(c + 1) * LANES]
        ahead = pltpu.roll(xs, LANES - HEAD_DIM // 2, 1)
        behind = pltpu.roll(xs, HEAD_DIM // 2, 1)
        outs.append(xs * cos + jnp.where(lo_half, ahead, behind) * sin_signed)
    return jnp.concatenate(outs, axis=1)


def _mix_in_core(x, g_ref, w_ref, cos_ref, sin_ref, lng_ref, lnb_ref, *, qk_w, attn_w, gm_w, q_scale):
    h = _rms(x, g_ref[...]).astype(BF16)
    cos = cos_ref[...]
    sin_signed = sin_ref[...]
    lane = lax.broadcasted_iota(jnp.int32, (1, LANES), 1)
    lo_half = (lane % HEAD_DIM) < (HEAD_DIM // 2)
    c0 = 0
    q = _rope(_dot(h, w_ref[:, c0:c0 + qk_w]), cos, sin_signed, lo_half) * q_scale
    c0 += qk_w
    k = _rope(_dot(h, w_ref[:, c0:c0 + qk_w]), cos, sin_signed, lo_half)
    c0 += qk_w
    v = _dot(h, w_ref[:, c0:c0 + attn_w])
    c0 += attn_w
    u = _gelu(_dot(h, w_ref[:, c0:c0 + gm_w]))
    c0 += gm_w
    gv = _gelu(_dot(h, w_ref[:, c0:c0 + gm_w]))
    mu = jnp.mean(gv, axis=-1, keepdims=True)
    gc = gv - mu
    vn = gc * lax.rsqrt(jnp.mean(gc * gc, axis=-1, keepdims=True) + NORM_EPS)
    return q, k, v, u, vn * lng_ref[...] + lnb_ref[...]


def _mix_in_prompt_body(x_ref, g_ref, w_ref, cos_ref, sin_ref, lng_ref, lnb_ref,
                        qt_ref, kt_ref, kb_ref, vf_ref, vt_ref, u_ref, vn_ref, **widths):
    q, k, v, u, vn = _mix_in_core(x_ref[0], g_ref, w_ref, cos_ref, sin_ref, lng_ref, lnb_ref, **widths)
    qt_ref[0] = q.T.astype(BF16)
    kt_ref[0] = k.T
    kb_ref[0] = k.astype(BF16)
    vf_ref[0] = v
    vt_ref[0] = v.T.astype(BF16)
    u_ref[0] = u
    vn_ref[0] = vn


def _mix_in_prompt(x3d, g_pre, w_in_bf, cos_t, sin_t, ln_g, ln_b, *, tm, qk_w, attn_w, gm_w):
    B, S, d_model = x3d.shape
    const = lambda b, j: (0, 0)
    rows = lambda b, j: (b, j, 0)
    cols = lambda b, j: (b, 0, j)

    def natural(width, dtype):
        return jax.ShapeDtypeStruct((B, S, width), dtype), pl.BlockSpec((1, tm, width), rows)

    def transposed(width, dtype):
        return jax.ShapeDtypeStruct((B, width, S), dtype), pl.BlockSpec((1, width, tm), cols)

    outs = [transposed(qk_w, BF16), transposed(qk_w, F32), natural(qk_w, BF16), natural(attn_w, F32),
            transposed(attn_w, BF16), natural(gm_w, F32), natural(gm_w, F32)]
    return pl.pallas_call(
        functools.partial(_mix_in_prompt_body, qk_w=qk_w, attn_w=attn_w, gm_w=gm_w,
                          q_scale=HEAD_DIM ** -0.5 * math.log2(math.e)),
        grid=(B, S // tm),
        in_specs=[
            pl.BlockSpec((1, tm, d_model), rows),
            pl.BlockSpec((1, d_model), const),
            pl.BlockSpec(w_in_bf.shape, const),
            pl.BlockSpec((tm, LANES), lambda b, j: (j, 0)),
            pl.BlockSpec((tm, LANES), lambda b, j: (j, 0)),
            pl.BlockSpec((1, gm_w), const),
            pl.BlockSpec((1, gm_w), const),
        ],
        out_specs=[o[1] for o in outs],
        out_shape=[o[0] for o in outs],
        compiler_params=pltpu.CompilerParams(dimension_semantics=("parallel", "parallel"),
                                             vmem_limit_bytes=VMEM_LIMIT),
        name="mix_in_prompt",
    )(x3d, g_pre, w_in_bf, cos_t, sin_t, ln_g, ln_b)


def _mix_in_sample_body(x_ref, g_ref, w_ref, cos_ref, sin_ref, lng_ref, lnb_ref,
                        q_ref, k_ref, v_ref, u_ref, vn_ref, **widths):
    q, k, v, u, vn = _mix_in_core(x_ref[...], g_ref, w_ref, cos_ref, sin_ref, lng_ref, lnb_ref, **widths)
    q_ref[...] = q.astype(BF16)
    k_ref[...] = k
    v_ref[...] = v
    u_ref[...] = u
    vn_ref[...] = vn


def _mix_in_sample(x2d, g_pre, w_in_bf, cos_t, sin_t, ln_g, ln_b, *, n_t, qk_w, attn_w, gm_w):
    n_dec = x2d.shape[0]
    d_model = w_in_bf.shape[0]
    const = lambda t: (0, 0)
    colblk = lambda t: (0, t)

    def out(width, dtype):
        return jax.ShapeDtypeStruct((n_dec, n_t * width), dtype), pl.BlockSpec((n_dec, width), colblk)

    outs = [out(qk_w, BF16), out(qk_w, F32), out(attn_w, F32), out(gm_w, F32), out(gm_w, F32)]
    return pl.pallas_call(
        functools.partial(_mix_in_sample_body, qk_w=qk_w, attn_w=attn_w, gm_w=gm_w, q_scale=HEAD_DIM ** -0.5),
        grid=(n_t,),
        in_specs=[
            pl.BlockSpec((n_dec, d_model), colblk),
            pl.BlockSpec((1, d_model), const),
            pl.BlockSpec(w_in_bf.shape, const),
            pl.BlockSpec((n_dec, LANES), lambda t: (t, 0)),
            pl.BlockSpec((n_dec, LANES), lambda t: (t, 0)),
            pl.BlockSpec((1, gm_w), const),
            pl.BlockSpec((1, gm_w), const),
        ],
        out_specs=[o[1] for o in outs],
        out_shape=[o[0] for o in outs],
        compiler_params=pltpu.CompilerParams(dimension_semantics=("parallel",),
                                             vmem_limit_bytes=VMEM_LIMIT),
        name="mix_in_sample",
    )(x2d, g_pre, w_in_bf, cos_t, sin_t, ln_g, ln_b)


def _prompt_attn_body(qi_ref, kj_ref, qt_ref, k_ref, vt_ref, lq1_ref, lk1_ref, lq2_ref, lk2_ref, sg_ref,
                      o_ref, m_sc, l_sc, a_sc, *, lam_init, q_slab):
    p = pl.program_id(2)
    qi = qi_ref[p]
    kj = kj_ref[p]
    tq = qt_ref.shape[2]
    tk = k_ref.shape[1]
    units = [(s, c) for s in range(2) for c in range(tq // q_slab)]

    @pl.when(kj == 0)
    def _():
        m_sc[...] = jnp.full(m_sc.shape, NEG_INF, F32)
        l_sc[...] = jnp.zeros(l_sc.shape, F32)
        a_sc[...] = jnp.zeros(a_sc.shape, F32)

    def block(diagonal):
        n_keys = (lambda c: (c + 1) * q_slab) if diagonal else (lambda c: tk)

        def scores(s, c):
            qs = qt_ref[0, :, c * q_slab:(c + 1) * q_slab]
            zeros = jnp.zeros((HEAD_DIM, q_slab), BF16)
            q_half = (jnp.concatenate([qs[:HEAD_DIM], zeros], axis=0) if s == 0
                      else jnp.concatenate([zeros, qs[HEAD_DIM:]], axis=0))
            return _dot(k_ref[0, :n_keys(c), :], q_half)

        def accumulate(st, s, c):
            cols = slice(c * q_slab, (c + 1) * q_slab)
            if diagonal:
                key = lax.broadcasted_iota(jnp.int32, st.shape, 0)
                qry = lax.broadcasted_iota(jnp.int32, st.shape, 1) + c * q_slab
                st = jnp.where(key <= qry, st, NEG_INF)
            m_prev = m_sc[s, :, cols]
            m_new = jnp.maximum(m_prev, jnp.max(st, axis=0, keepdims=True))
            alpha = jnp.exp2(m_prev - m_new)
            pt = jnp.exp2(st - m_new)
            l_sc[s, :, cols] = alpha * l_sc[s, :, cols] + jnp.sum(pt, axis=0, keepdims=True)
            a_sc[s, :, cols] = alpha * a_sc[s, :, cols] + _dot(vt_ref[0, :, :n_keys(c)], pt.astype(BF16))
            m_sc[s, :, cols] = m_new

        st_next = scores(*units[0])
        for u, (s, c) in enumerate(units):
            st = st_next
            if u + 1 < len(units):
                st_next = scores(*units[u + 1])
            accumulate(st, s, c)

    @pl.when(kj < qi)
    def _():
        block(False)

    @pl.when(kj == qi)
    def _():
        block(True)
        lam = _diff_lambda(lq1_ref[...], lk1_ref[...], lq2_ref[...], lk2_ref[...], lam_init)
        ot = a_sc[0] / l_sc[0] - lam * (a_sc[1] / l_sc[1])
        ms = jnp.mean(ot * ot, axis=0, keepdims=True)
        y = ot * lax.rsqrt(ms + NORM_EPS) * sg_ref[...] * (1.0 - lam_init)
        o_ref[0] = y.T


def _prompt_attn(qt, k, vt, lq1, lk1, lq2, lk2, subln_g_col, *, lam_init, n_heads, tq):
    B, S, _ = k.shape
    nq = S // tq
    pairs = [(i, j) for i in range(nq) for j in range(i + 1)]
    qi = jnp.asarray([p[0] for p in pairs], jnp.int32)
    kj = jnp.asarray([p[1] for p in pairs], jnp.int32)
    const = lambda b, h, p, qi, kj: (0, 0)
    grid_spec = pltpu.PrefetchScalarGridSpec(
        num_scalar_prefetch=2,
        grid=(B, n_heads, len(pairs)),
        in_specs=[
            pl.BlockSpec((1, V_DIM, tq), lambda b, h, p, qi, kj: (b, h, qi[p])),
            pl.BlockSpec((1, tq, V_DIM), lambda b, h, p, qi, kj: (b, kj[p], h)),
            pl.BlockSpec((1, V_DIM, tq), lambda b, h, p, qi, kj: (b, h, kj[p])),
            pl.BlockSpec((1, HEAD_DIM), const),
            pl.BlockSpec((1, HEAD_DIM), const),
            pl.BlockSpec((1, HEAD_DIM), const),
            pl.BlockSpec((1, HEAD_DIM), const),
            pl.BlockSpec((V_DIM, 1), const),
        ],
        out_specs=pl.BlockSpec((1, tq, V_DIM), lambda b, h, p, qi, kj: (b, qi[p], h)),
        scratch_shapes=[pltpu.VMEM((2, 1, tq), F32), pltpu.VMEM((2, 1, tq), F32), pltpu.VMEM((2, V_DIM, tq), F32)],
    )
    return pl.pallas_call(
        functools.partial(_prompt_attn_body, lam_init=lam_init, q_slab=ATTN_Q_SLAB),
        grid_spec=grid_spec,
        out_shape=jax.ShapeDtypeStruct((B, S, n_heads * V_DIM), F32),
        compiler_params=pltpu.CompilerParams(
            dimension_semantics=("parallel", "parallel", "arbitrary"), vmem_limit_bytes=VMEM_LIMIT),
        name="prompt_attn",
    )(qi, kj, qt, k, vt, lq1, lk1, lq2, lk2, subln_g_col)


def _paged_init(m_sc, l_sc, acc_sc):
    m_sc[...] = jnp.full(m_sc.shape, NEG_INF, F32)
    l_sc[...] = jnp.zeros(l_sc.shape, F32)
    acc_sc[...] = jnp.zeros(acc_sc.shape, F32)


def _paged_scores(qbd, k_refs):
    return jnp.concatenate([_dot(qbd, r[0].astype(BF16)) for r in k_refs], axis=1)


def _paged_accumulate(s, t, v_refs, m_sc, l_sc, acc_sc, *, n_heads):
    page = v_refs[0].shape[1] // n_heads
    m_prev = m_sc[t]
    m_new = jnp.maximum(m_prev, jnp.max(s, axis=-1, keepdims=True))
    alpha = jnp.exp(m_prev - m_new)
    pr = jnp.exp(s - m_new)
    l_sc[t] = alpha * l_sc[t] + jnp.sum(pr, axis=-1, keepdims=True)
    prb = pr.astype(BF16)
    for h in range(n_heads):
        cols = slice(h * V_DIM, (h + 1) * V_DIM)
        v_h = jnp.concatenate([r[pl.ds(0, 1), pl.ds(h, page, stride=n_heads), :][0] for r in v_refs],
                              axis=0).astype(BF16)
        acc_sc[t, :, cols] = alpha * acc_sc[t, :, cols] + _dot(prb, v_h)
    m_sc[t] = m_new


def _paged_finish(qbd, kn_ref, vn_ref, lam, sg_ref, o_ref, m_sc, l_sc, acc_sc, y_sc, *, lam_init, n_heads):
    n_streams, n_rows, _ = acc_sc.shape
    n_new = kn_ref.shape[1]
    half = n_rows // 2
    qf = qbd.astype(F32)
    q_tok = lax.broadcasted_iota(jnp.int32, (n_rows, 1), 0) % n_new
    s_new = []
    for j in range(n_new):
        kj = kn_ref[0, j:j + 1, :].astype(BF16).astype(F32)
        s_new.append(jnp.where(q_tok >= j, jnp.sum(qf * kj, axis=-1, keepdims=True), NEG_INF))
    m_fin = m_sc[0]
    for t in range(1, n_streams):
        m_fin = jnp.maximum(m_fin, m_sc[t])
    for sj in s_new:
        m_fin = jnp.maximum(m_fin, sj)
    l_fin = jnp.zeros_like(m_fin)
    acc = jnp.zeros(acc_sc.shape[1:], F32)
    for t in range(n_streams):
        alpha = jnp.exp(m_sc[t] - m_fin)
        l_fin = l_fin + alpha * l_sc[t]
        acc = acc + alpha * acc_sc[t]
    for j, sj in enumerate(s_new):
        pj = jnp.exp(sj - m_fin)
        l_fin = l_fin + pj
        acc = acc + pj.astype(BF16).astype(F32) * vn_ref[0, j:j + 1, :].astype(BF16).astype(F32)
    o = acc / l_fin
    o = o[:half] - lam * o[half:]
    for h in range(n_heads):
        blk = o[:, h * V_DIM:(h + 1) * V_DIM]
        y_sc[:, h * V_DIM:(h + 1) * V_DIM] = _rms(blk, sg_ref[...]) * (1.0 - lam_init)
    for h in range(n_heads):
        o_ref[0, :, h * V_DIM:(h + 1) * V_DIM] = y_sc[h * n_new:(h + 1) * n_new, h * V_DIM:(h + 1) * V_DIM]


def _mix_out_prompt_body(x_ref, o_ref, u_ref, vn_ref, ws_ref, bs_ref, wout_ref, gpost_ref, y_ref, gate_sc):
    tm = x_ref.shape[0]
    attn_w = o_ref.shape[1]
    gd = vn_ref.shape[1] // N_GROUPS
    row = lax.broadcasted_iota(jnp.int32, (CHUNK, CHUNK), 0)
    col = lax.broadcasted_iota(jnp.int32, (CHUNK, CHUNK), 1)
    for g in range(N_GROUPS):
        wg = jnp.where(col <= row, ws_ref[g], 0.0).astype(BF16)
        bias = bs_ref[:, g * gd:(g + 1) * gd]
        for c in range(tm // CHUNK):
            rows = slice(c * CHUNK, (c + 1) * CHUNK)
            cols = slice(g * gd, (g + 1) * gd)
            s = _dot(wg, vn_ref[rows, cols].astype(BF16)) + bias
            gate_sc[rows, cols] = (u_ref[rows, cols] * s).astype(BF16)
    m = _dot(o_ref[...].astype(BF16), wout_ref[:attn_w, :]) + _dot(gate_sc[...], wout_ref[attn_w:, :])
    y_ref[...] = x_ref[...] + _rms(m, gpost_ref[...])


def _mix_out_prompt(x2d, o2d, u2d, vn2d, w_spatial, bs_exp, w_out_bf, g_post, *, tm):
    n_rows, d_model = x2d.shape
    attn_w = o2d.shape[1]
    gm_w = u2d.shape[1]
    const2 = lambda i: (0, 0)
    rows = lambda i: (i, 0)
    return pl.pallas_call(
        _mix_out_prompt_body,
        grid=(n_rows // tm,),
        in_specs=[
            pl.BlockSpec((tm, d_model), rows),
            pl.BlockSpec((tm, attn_w), rows),
            pl.BlockSpec((tm, gm_w), rows),
            pl.BlockSpec((tm, gm_w), rows),
            pl.BlockSpec(w_spatial.shape, lambda i: (0, 0, 0)),
            pl.BlockSpec(bs_exp.shape, const2),
            pl.BlockSpec(w_out_bf.shape, const2),
            pl.BlockSpec((1, d_model), const2),
        ],
        out_specs=pl.BlockSpec((tm, d_model), rows),
        out_shape=jax.ShapeDtypeStruct((n_rows, d_model), F32),
        scratch_shapes=[pltpu.VMEM((tm, gm_w), BF16)],
        compiler_params=pltpu.CompilerParams(dimension_semantics=("parallel",),
                                             vmem_limit_bytes=VMEM_LIMIT),
        name="mix_out_prompt",
    )(x2d, o2d, u2d, vn2d, w_spatial, bs_exp, w_out_bf, g_post)


def _mix_out_sample_body(x_ref, o_ref, u_ref, vn_ref, coef_ref, bias_ref, wout_ref, gpost_ref, y_ref, hist):
    t = pl.program_id(0)
    n_t = hist.shape[0]
    attn_w = o_ref.shape[1]

    @pl.when(t == 0)
    def _():
        hist[...] = jnp.zeros(hist.shape, F32)

    hist[t] = vn_ref[...]
    s = bias_ref[pl.ds(t, 1), :]
    for j in range(n_t):
        cj = jnp.where(j <= t, coef_ref[pl.ds(t * n_t + j, 1), :], 0.0)
        s = s + cj.astype(BF16).astype(F32) * hist[j].astype(BF16).astype(F32)
    gate = (u_ref[...] * s).astype(BF16)
    m = _dot(o_ref[...].astype(BF16), wout_ref[:attn_w, :]) + _dot(gate, wout_ref[attn_w:, :])
    y_ref[...] = x_ref[...] + _rms(m, gpost_ref[...])


def _mix_out_sample(x2d, o2d, u2d, vn2d, coef, bias, w_out_bf, g_post, *, n_t):
    n_dec = x2d.shape[0]
    d_model = x2d.shape[1] // n_t
    attn_w = o2d.shape[1] // n_t
    gm_w = u2d.shape[1] // n_t
    const2 = lambda t: (0, 0)
    colblk = lambda t: (0, t)
    return pl.pallas_call(
        _mix_out_sample_body,
        grid=(n_t,),
        in_specs=[
            pl.BlockSpec((n_dec, d_model), colblk),
            pl.BlockSpec((n_dec, attn_w), colblk),
            pl.BlockSpec((n_dec, gm_w), colblk),
            pl.BlockSpec((n_dec, gm_w), colblk),
            pl.BlockSpec(coef.shape, const2),
            pl.BlockSpec(bias.shape, const2),
            pl.BlockSpec(w_out_bf.shape, const2),
            pl.BlockSpec((1, d_model), const2),
        ],
        out_specs=pl.BlockSpec((n_dec, d_model), colblk),
        out_shape=jax.ShapeDtypeStruct(x2d.shape, F32),
        scratch_shapes=[pltpu.VMEM((n_t, n_dec, gm_w), F32)],
        compiler_params=pltpu.CompilerParams(dimension_semantics=("arbitrary",),
                                             vmem_limit_bytes=VMEM_LIMIT),
        name="mix_out_sample",
    )(x2d, o2d, u2d, vn2d, coef, bias, w_out_bf, g_post)


def _ffn_chunks(h, taps, cw_ref, cb_ref, wup_ref, wdn_ref, *, d_ff, col_chunk, chunk_ids, between=()):
    gate_cols = lambda c: slice(c * col_chunk, (c + 1) * col_chunk)
    val_cols = lambda c: slice(d_ff + c * col_chunk, d_ff + (c + 1) * col_chunk)

    def up(c):
        return _dot(h, wup_ref[:, gate_cols(c)]), _dot(h, wup_ref[:, val_cols(c)])

    def conv(u, cols):
        shifted = taps(u, cols)
        y = cb_ref[:, cols] + cw_ref[0:1, cols] * shifted[0]
        for j in range(1, CONV_W):
            y = y + cw_ref[j:j + 1, cols] * shifted[j]
        return y

    def down(a, c):
        return _dot(a, wdn_ref[gate_cols(c), :])

    acc = None
    act = None
    up_next = up(chunk_ids[0])
    for i, c in enumerate(chunk_ids):
        up_gate, up_val = up_next
        if i + 1 < len(chunk_ids):
            up_next = up(chunk_ids[i + 1])
        if act is not None:
            part = down(act, chunk_ids[i - 1])
            acc = part if acc is None else acc + part
        act = (_gelu(conv(up_gate, gate_cols(c))) * conv(up_val, val_cols(c))).astype(BF16)
        if i < len(between):
            between[i]()
    part = down(act, chunk_ids[-1])
    return part if acc is None else acc + part


def _ffn_prompt_body(pt_ref, x_ref, g_ref, wup_ref, cw_ref, cb_ref, wdn_ref, gpost_ref,
                     qbd_ref, kn_ref, vn_ref, lq1_ref, lk1_ref, lq2_ref, lk2_ref, sg_ref, *rest,
                     d_ff, col_chunk, lam_init, n_pages_step, n_streams, n_heads, chunk_groups):
    k_refs = rest[:n_pages_step]
    v_refs = rest[n_pages_step:2 * n_pages_step]
    y_ref, conv_ref, o_ref = rest[2 * n_pages_step:2 * n_pages_step + 3]
    carry1, carry2, h_sc, acc_ffn, m_sc, l_sc, acc_sc, y_sc = rest[2 * n_pages_step + 3:]
    j = pl.program_id(1)
    sub = pl.program_id(2)
    per = n_pages_step // n_streams
    row = lax.broadcasted_iota(jnp.int32, (SUBLANES, 1), 0)

    @pl.when((j == 0) & (sub == 0))
    def _():
        carry1[...] = jnp.zeros(carry1.shape, F32)
        carry2[...] = jnp.zeros(carry2.shape, F32)

    def taps(up, cols):
        r1 = pltpu.roll(up, 1, 0)
        r2 = pltpu.roll(up, 2, 0)
        t1 = jnp.concatenate([jnp.where(row < 1, carry1[:, cols], r1[:SUBLANES]), r1[SUBLANES:]], axis=0)
        t2 = jnp.concatenate([jnp.where(row < 2, carry2[:, cols], r2[:SUBLANES]), r2[SUBLANES:]], axis=0)
        carry1[:, cols] = r1[:SUBLANES]
        carry2[:, cols] = r2[:SUBLANES]
        conv_ref[0, :, cols] = r2[:CONV_W - 1]
        return [t2, t1, up]

    def sub_step(grp, chunk_ids):
        first, last = grp == 0, grp == len(chunk_groups) - 1
        qbd = qbd_ref[0]
        if first:
            h = _rms(x_ref[0], g_ref[...]).astype(BF16)
            h_sc[...] = h
            _paged_init(m_sc, l_sc, acc_sc)
        else:
            h = h_sc[...]
        scores = [None] * n_streams
        scores[0] = _paged_scores(qbd, k_refs[:per])

        def stream(t):
            def run():
                _paged_accumulate(scores[t], t, v_refs[t * per:(t + 1) * per], m_sc, l_sc, acc_sc, n_heads=n_heads)
                if t + 1 < n_streams:
                    scores[t + 1] = _paged_scores(qbd, k_refs[(t + 1) * per:(t + 2) * per])
            return run

        n_between = min(n_streams, len(chunk_ids))
        part = _ffn_chunks(h, taps, cw_ref, cb_ref, wup_ref, wdn_ref, d_ff=d_ff, col_chunk=col_chunk,
                           chunk_ids=chunk_ids, between=[stream(t) for t in range(n_between)])
        for t in range(n_between, n_streams):
            stream(t)()
        if first:
            acc_ffn[...] = part
        elif not last:
            acc_ffn[...] = acc_ffn[...] + part
        if last:
            total = part if first else acc_ffn[...] + part
            y_ref[0] = x_ref[0] + _rms(total, gpost_ref[...])
            lam = _diff_lambda(lq1_ref[...], lk1_ref[...], lq2_ref[...], lk2_ref[...], lam_init)
            _paged_finish(qbd, kn_ref, vn_ref, lam, sg_ref, o_ref, m_sc, l_sc, acc_sc, y_sc,
                          lam_init=lam_init, n_heads=n_heads)

    for grp, chunk_ids in enumerate(chunk_groups):
        pl.when(sub == grp)(functools.partial(sub_step, grp, chunk_ids))


def _ffn_prompt_paged(x3d, g_pre, w_up_bf, conv_w, conv_b, w_down_bf, g_post,
                      page_table, qbd, k_new, v_new, cache_kt, cache_v, lq1, lk1, lq2, lk2, subln_g, *,
                      tm, col_chunk, lam_init, n_heads, n_pages_step, n_streams):
    B, S, d_model = x3d.shape
    d_ff = w_down_bf.shape[0]
    n_dec, n_pages = page_table.shape
    _, width, page = cache_kt.shape
    n_new = k_new.shape[1]
    n_rows = qbd.shape[1]
    G = n_pages_step
    tiles = S // tm
    n_sub = n_pages // G
    n_chunks = d_ff // col_chunk
    assert B * tiles == n_dec and n_pages % G == 0 and G % n_streams == 0 and n_sub <= n_chunks
    bounds = [round(i * n_chunks / n_sub) for i in range(n_sub + 1)]
    chunk_groups = tuple(tuple(range(bounds[i], bounds[i + 1])) for i in range(n_sub))

    const2 = lambda b, j, s, pt: (0, 0)
    tile = lambda b, j, s, pt: (b, j, 0)
    per_seq = lambda b, j, s, pt: (b * tiles + j, 0, 0)

    def page_spec(shape, jj):
        return pl.BlockSpec(shape, lambda b, j, s, pt: (pt[b * tiles + j, s * G + jj], 0, 0))

    grid_spec = pltpu.PrefetchScalarGridSpec(
        num_scalar_prefetch=1,
        grid=(B, tiles, n_sub),
        in_specs=[
            pl.BlockSpec((1, tm, d_model), tile),
            pl.BlockSpec((1, d_model), const2),
            pl.BlockSpec(w_up_bf.shape, const2, pipeline_mode=pl.Buffered(1)),
            pl.BlockSpec(conv_w.shape, const2),
            pl.BlockSpec(conv_b.shape, const2),
            pl.BlockSpec(w_down_bf.shape, const2, pipeline_mode=pl.Buffered(1)),
            pl.BlockSpec((1, d_model), const2),
            pl.BlockSpec((1, n_rows, width), per_seq),
            pl.BlockSpec((1, n_new, width), per_seq),
            pl.BlockSpec((1, n_new, width), per_seq),
            pl.BlockSpec((1, HEAD_DIM), const2),
            pl.BlockSpec((1, HEAD_DIM), const2),
            pl.BlockSpec((1, HEAD_DIM), const2),
            pl.BlockSpec((1, HEAD_DIM), const2),
            pl.BlockSpec((1, V_DIM), const2),
        ] + [page_spec((1, width, page), jj) for jj in range(G)]
          + [page_spec((1, page * n_heads, V_DIM), jj) for jj in range(G)],
        out_specs=[
            pl.BlockSpec((1, tm, d_model), tile),
            pl.BlockSpec((1, CONV_W - 1, 2 * d_ff), lambda b, j, s, pt: (b, 0, 0)),
            pl.BlockSpec((1, n_new, width), per_seq),
        ],
        scratch_shapes=[
            pltpu.VMEM((SUBLANES, 2 * d_ff), F32), pltpu.VMEM((SUBLANES, 2 * d_ff), F32),
            pltpu.VMEM((tm, d_model), BF16), pltpu.VMEM((tm, d_model), F32),
            pltpu.VMEM((n_streams, n_rows, 1), F32), pltpu.VMEM((n_streams, n_rows, 1), F32),
            pltpu.VMEM((n_streams, n_rows, width), F32), pltpu.VMEM((n_rows // 2, width), F32),
        ],
    )
    return pl.pallas_call(
        functools.partial(_ffn_prompt_body, d_ff=d_ff, col_chunk=col_chunk, lam_init=lam_init, n_pages_step=G,
                          n_streams=n_streams, n_heads=n_heads, chunk_groups=chunk_groups),
        grid_spec=grid_spec,
        out_shape=[
            jax.ShapeDtypeStruct((B, S, d_model), F32),
            jax.ShapeDtypeStruct((B, CONV_W - 1, 2 * d_ff), F32),
            jax.ShapeDtypeStruct((n_dec, n_new, width), F32),
        ],
        compiler_params=pltpu.CompilerParams(dimension_semantics=("parallel", "arbitrary", "arbitrary"),
                                             vmem_limit_bytes=VMEM_LIMIT),
        name="ffn_prompt_paged",
    )(page_table, x3d, g_pre, w_up_bf, conv_w, conv_b, w_down_bf, g_post,
      qbd, k_new, v_new, lq1, lk1, lq2, lk2, subln_g, *([cache_kt] * G), *([cache_v] * G))


def _ffn_sample_body(x_ref, st_ref, g_ref, wup_ref, cw_ref, cb_ref, wdn_ref, gpost_ref, y_ref, conv_ref, ring, *,
                     d_ff, col_chunk):
    t = pl.program_id(0)
    width = 2 * d_ff

    @pl.when(t == 0)
    def _():
        for j in range(CONV_W - 1):
            ring[j] = st_ref[:, j * width:(j + 1) * width]

    x = x_ref[...]
    h = _rms(x, g_ref[...]).astype(BF16)
    cur = (t + CONV_W - 1) % CONV_W

    def taps(up, cols):
        ring[cur, :, cols] = up
        conv_ref[:, cols] = up
        return [ring[(t + j) % CONV_W, :, cols] for j in range(CONV_W - 1)] + [up]

    acc = _ffn_chunks(h, taps, cw_ref, cb_ref, wup_ref, wdn_ref, d_ff=d_ff, col_chunk=col_chunk,
                      chunk_ids=tuple(range(d_ff // col_chunk)))
    y_ref[...] = x + _rms(acc, gpost_ref[...])


def _ffn_sample(x2d, state2d, g_pre, w_up_bf, conv_w, conv_b, w_down_bf, g_post, *, n_t, col_chunk):
    n_dec = x2d.shape[0]
    d_model = x2d.shape[1] // n_t
    d_ff = w_down_bf.shape[0]
    width = 2 * d_ff
    const2 = lambda t: (0, 0)
    colblk = lambda t: (0, t)
    conv_map = lambda t: (0, jnp.maximum(t - (n_t - (CONV_W - 1)), 0))
    return pl.pallas_call(
        functools.partial(_ffn_sample_body, d_ff=d_ff, col_chunk=col_chunk),
        grid=(n_t,),
        in_specs=[
            pl.BlockSpec((n_dec, d_model), colblk),
            pl.BlockSpec(state2d.shape, const2),
            pl.BlockSpec((1, d_model), const2),
            pl.BlockSpec(w_up_bf.shape, const2, pipeline_mode=pl.Buffered(1)),
            pl.BlockSpec(conv_w.shape, const2),
            pl.BlockSpec(conv_b.shape, const2),
            pl.BlockSpec(w_down_bf.shape, const2, pipeline_mode=pl.Buffered(1)),
            pl.BlockSpec((1, d_model), const2),
        ],
        out_specs=[
            pl.BlockSpec((n_dec, d_model), colblk),
            pl.BlockSpec((n_dec, width), conv_map),
        ],
        out_shape=[
            jax.ShapeDtypeStruct(x2d.shape, F32),
            jax.ShapeDtypeStruct((n_dec, (CONV_W - 1) * width), F32),
        ],
        scratch_shapes=[pltpu.VMEM((CONV_W, n_dec, width), F32)],
        compiler_params=pltpu.CompilerParams(dimension_semantics=("arbitrary",),
                                             vmem_limit_bytes=VMEM_LIMIT),
        name="ffn_sample",
    )(x2d, state2d, g_pre, w_up_bf, conv_w, conv_b, w_down_bf, g_post)


def _rope_tables(pos):
    half = HEAD_DIM // 2
    inv = ROPE_THETA ** (-jnp.arange(half, dtype=F32) * 2.0 / HEAD_DIM)
    ang = pos[:, None] * inv[None, :]
    cos = jnp.cos(ang)
    sin = jnp.sin(ang)
    cos_t = jnp.concatenate([cos, cos, cos, cos], axis=1)
    sin_t = jnp.concatenate([-sin, sin, -sin, sin], axis=1)
    return cos_t, sin_t


def kernel(x_prompt, x_sample, cache_k, cache_v, state_conv, page_table, norm_mix_pre, w_in, lambda_q1,
           lambda_k1, lambda_q2, lambda_k2, subln_g, gate_ln_g, gate_ln_b, w_spatial, b_spatial, w_out,
           norm_mix_post, norm_ffn_pre, w_up, conv_w, conv_b, w_down, norm_ffn_post):
    B, S, d_model = x_prompt.shape
    n_dec, n_t, _ = x_sample.shape
    depth, n_phys, page, n_heads, _, _ = cache_k.shape
    qk_w = n_heads * 2 * HEAD_DIM
    attn_w = n_heads * V_DIM
    gm_w = (w_in.shape[2] - 2 * qk_w - attn_w) // 2
    d_ff = w_down.shape[1]
    past_len = page_table.shape[1] * page
    assert depth == 1 and w_spatial.shape[1] == N_GROUPS and w_spatial.shape[2] == CHUNK
    assert CONV_W - 1 <= n_t <= CHUNK and conv_w.shape[1] == CONV_W
    widths = dict(qk_w=qk_w, attn_w=attn_w, gm_w=gm_w)

    l = 0
    lam_init = 0.8 - 0.6 * math.exp(-0.3 * l)
    row = lambda a: a[l].reshape(1, -1)
    w_in_bf = w_in[l].astype(BF16)
    w_out_bf = w_out[l].astype(BF16)
    w_up_bf = w_up[l].astype(BF16)
    w_down_bf = w_down[l].astype(BF16)
    lq1, lk1, lq2, lk2 = row(lambda_q1), row(lambda_k1), row(lambda_q2), row(lambda_k2)
    sg = row(subln_g)
    g_mix_pre, g_mix_post = row(norm_mix_pre), row(norm_mix_post)
    g_ffn_pre, g_ffn_post = row(norm_ffn_pre), row(norm_ffn_post)
    ln_g, ln_b = row(gate_ln_g), row(gate_ln_b)
    conv_b2 = row(conv_b)
    gd = gm_w // N_GROUPS

    cos_p, sin_p = _rope_tables(jnp.arange(S, dtype=F32))
    qt_p, kt_p, kb_p, vf_p, vt_p, u_p, vn_p = _mix_in_prompt(
        x_prompt, g_mix_pre, w_in_bf, cos_p, sin_p, ln_g, ln_b, tm=TILE_MIX_IN, **widths)
    o_p = _prompt_attn(qt_p, kb_p, vt_p, lq1, lk1, lq2, lk2, sg.reshape(V_DIM, 1),
                       lam_init=lam_init, n_heads=n_heads, tq=TILE_ATTN)
    bs_exp = jnp.repeat(b_spatial[l].T, gd, axis=1)
    x1_p = _mix_out_prompt(x_prompt.reshape(B * S, d_model), o_p.reshape(B * S, attn_w),
                           u_p.reshape(B * S, gm_w), vn_p.reshape(B * S, gm_w), w_spatial[l], bs_exp, w_out_bf,
                           g_mix_post, tm=TILE_MIX_OUT)
    new_k_p = kt_p.reshape(B, n_heads, 2, HEAD_DIM, S).transpose(0, 4, 1, 2, 3)[None]

    cos_s, sin_s = _rope_tables(jnp.arange(n_t, dtype=F32) + float(past_len))
    cos_s = jnp.repeat(cos_s, n_dec, axis=0)
    sin_s = jnp.repeat(sin_s, n_dec, axis=0)
    xs2d = x_sample.reshape(n_dec, n_t * d_model)
    q_s, kf_s, vf_s, u_s, vn_s = _mix_in_sample(xs2d, g_mix_pre, w_in_bf, cos_s, sin_s, ln_g, ln_b, n_t=n_t,
                                                **widths)
    q_hs = q_s.reshape(n_dec, n_t, n_heads, 2, HEAD_DIM).transpose(0, 3, 2, 1, 4)
    same = ((jnp.arange(2)[:, None, None, None] == jnp.arange(2)[None, None, None, :])
            & (jnp.arange(n_heads)[None, :, None, None] == jnp.arange(n_heads)[None, None, :, None]))
    qbd = jnp.where(same[None, :, :, None, :, :, None], q_hs[:, :, :, :, None, None, :], jnp.zeros((), BF16))
    qbd = qbd.reshape(n_dec, 2 * n_heads * n_t, qk_w)
    cache_kt = cache_k[l].transpose(0, 2, 3, 4, 1).reshape(n_phys, qk_w, page)
    cache_v2 = cache_v[l].reshape(n_phys, page * n_heads, V_DIM)

    y_p, conv_p, o_s = _ffn_prompt_paged(
        x1_p.reshape(B, S, d_model), g_ffn_pre, w_up_bf, conv_w[l], conv_b2, w_down_bf, g_ffn_post,
        page_table, qbd, kf_s.reshape(n_dec, n_t, qk_w), vf_s.reshape(n_dec, n_t, attn_w), cache_kt, cache_v2,
        lq1, lk1, lq2, lk2, sg, tm=TILE_FFN, col_chunk=FFN_COL_CHUNK, lam_init=lam_init, n_heads=n_heads,
        n_pages_step=PAGES_PER_STEP, n_streams=PAGE_STREAMS)

    coef = jnp.repeat(w_spatial[l][:, :n_t, :n_t].transpose(1, 2, 0).reshape(n_t * n_t, N_GROUPS), gd, axis=1)
    bias = jnp.repeat(b_spatial[l][:, :n_t].T, gd, axis=1)
    x1_s = _mix_out_sample(xs2d, o_s.reshape(n_dec, n_t * attn_w), u_s, vn_s, coef, bias, w_out_bf, g_mix_post,
                           n_t=n_t)
    y_s, conv_s = _ffn_sample(x1_s, state_conv[l].reshape(n_dec, (CONV_W - 1) * 2 * d_ff), g_ffn_pre, w_up_bf,
                              conv_w[l], conv_b2, w_down_bf, g_ffn_post, n_t=n_t, col_chunk=FFN_COL_CHUNK)

    return (y_p,
            y_s.reshape(n_dec, n_t, d_model),
            new_k_p,
            vf_p.reshape(1, B, S, n_heads, V_DIM),
            conv_p.reshape(1, B, CONV_W - 1, 2 * d_ff),
            kf_s.reshape(1, n_dec, n_t, n_heads, 2, HEAD_DIM),
            vf_s.reshape(1, n_dec, n_t, n_heads, V_DIM),
            conv_s.reshape(1, n_dec, CONV_W - 1, 2 * d_ff),
            vn_s.reshape(1, n_dec, n_t, gm_w))
```

```python
import functools
import math

import jax
import jax.numpy as jnp
from jax import lax
from jax.experimental import pallas as pl
from jax.experimental.pallas import tpu as pltpu

F32 = jnp.float32
BF16 = jnp.bfloat16

HEAD_DIM = 64
V_DIM = 2 * HEAD_DIM
CHUNK = 128
N_GROUPS = 4
CONV_W = 3
ROPE_THETA = 10000.0
NORM_EPS = 1e-6
NEG_INF = -1e30
LANES = 128
SUBLANES = 8
VMEM_LIMIT = 56 * 1024 * 1024

TILE_MIX_IN = 512
TILE_ATTN = 2048
ATTN_Q_SLAB = 256
TILE_MIX_OUT = 512
TILE_FFN = 256
FFN_COL_CHUNK = 256
PAGES_PER_STEP = 16
PAGE_STREAMS = 2


def _rms(x, g):
    return x * lax.rsqrt(jnp.mean(x * x, axis=-1, keepdims=True) + NORM_EPS) * g


def _gelu(x):
    c = math.sqrt(2.0 / math.pi)
    hx = 0.5 * x
    return hx + hx * jnp.tanh(x * (c + (c * 0.044715) * (x * x)))


def _dot(a, b):
    return jnp.dot(a, b, preferred_element_type=F32)


def _diff_lambda(lq1, lk1, lq2, lk2, lam_init):
    a = jnp.sum(lq1 * lk1, axis=-1, keepdims=True)
    b = jnp.sum(lq2 * lk2, axis=-1, keepdims=True)
    return jnp.exp(a) - jnp.exp(b) + lam_init


def _rope(z, cos, sin_signed, lo_half):
    outs = []
    for c in range(z.shape[1] // LANES):
        xs = z[:, c * LANES:(c + 1) * LANES]
        ahead = pltpu.roll(xs, LANES - HEAD_DIM // 2, 1)
        behind = pltpu.roll(xs, HEAD_DIM // 2, 1)
        outs.append(xs * cos + jnp.where(lo_half, ahead, behind) * sin_signed)
    return jnp.concatenate(outs, axis=1)


def _mix_in_core(x, g_ref, w_ref, cos_ref, sin_ref, lng_ref, lnb_ref, *, qk_w, attn_w, gm_w, q_scale):
    h = _rms(x, g_ref[...]).astype(BF16)
    cos = cos_ref[...]
    sin_signed = sin_ref[...]
    lane = lax.broadcasted_iota(jnp.int32, (1, LANES), 1)
    lo_half = (lane % HEAD_DIM) < (HEAD_DIM // 2)
    c0 = 0
    q = _rope(_dot(h, w_ref[:, c0:c0 + qk_w]), cos, sin_signed, lo_half) * q_scale
    c0 += qk_w
    k = _rope(_dot(h, w_ref[:, c0:c0 + qk_w]), cos, sin_signed, lo_half)
    c0 += qk_w
    v = _dot(h, w_ref[:, c0:c0 + attn_w])
    c0 += attn_w
    u = _gelu(_dot(h, w_ref[:, c0:c0 + gm_w]))
    c0 += gm_w
    gv = _gelu(_dot(h, w_ref[:, c0:c0 + gm_w]))
    mu = jnp.mean(gv, axis=-1, keepdims=True)
    gc = gv - mu
    vn = gc * lax.rsqrt(jnp.mean(gc * gc, axis=-1, keepdims=True) + NORM_EPS)
    return q, k, v, u, vn * lng_ref[...] + lnb_ref[...]


def _mix_in_prompt_body(x_ref, g_ref, w_ref, cos_ref, sin_ref, lng_ref, lnb_ref,
                        qt_ref, kt_ref, kb_ref, vf_ref, vt_ref, u_ref, vn_ref, **widths):
    q, k, v, u, vn = _mix_in_core(x_ref[0], g_ref, w_ref, cos_ref, sin_ref, lng_ref, lnb_ref, **widths)
    qt_ref[0] = q.T.astype(BF16)
    kt_ref[0] = k.T
    kb_ref[0] = k.astype(BF16)
    tm = v.shape[0]
    n_heads = v.shape[1] // V_DIM
    for h in range(n_heads):
        vf_ref[pl.ds(0, 1), pl.ds(h, tm, stride=n_heads), :] = v[:, h * V_DIM:(h + 1) * V_DIM][None]
    vt_ref[0] = v.T.astype(BF16)
    u_ref[0] = u
    vn_ref[0] = vn.astype(BF16)


def _mix_in_prompt(x3d, g_pre, w_in_bf, cos_t, sin_t, ln_g, ln_b, *, tm, qk_w, attn_w, gm_w):
    B, S, d_model = x3d.shape
    const = lambda b, j: (0, 0)
    rows = lambda b, j: (b, j, 0)
    cols = lambda b, j: (b, 0, j)

    def natural(width, dtype):
        return jax.ShapeDtypeStruct((B, S, width), dtype), pl.BlockSpec((1, tm, width), rows)

    def transposed(width, dtype):
        return jax.ShapeDtypeStruct((B, width, S), dtype), pl.BlockSpec((1, width, tm), cols)

    n_heads = attn_w // V_DIM
    v_rows = (jax.ShapeDtypeStruct((B, S * n_heads, V_DIM), F32), pl.BlockSpec((1, tm * n_heads, V_DIM), rows))
    outs = [transposed(qk_w, BF16), transposed(qk_w, F32), natural(qk_w, BF16), v_rows,
            transposed(attn_w, BF16), natural(gm_w, F32), natural(gm_w, BF16)]
    return pl.pallas_call(
        functools.partial(_mix_in_prompt_body, qk_w=qk_w, attn_w=attn_w, gm_w=gm_w,
                          q_scale=HEAD_DIM ** -0.5 * math.log2(math.e)),
        grid=(B, S // tm),
        in_specs=[
            pl.BlockSpec((1, tm, d_model), rows),
            pl.BlockSpec((1, d_model), const),
            pl.BlockSpec(w_in_bf.shape, const),
            pl.BlockSpec((tm, LANES), lambda b, j: (j, 0)),
            pl.BlockSpec((tm, LANES), lambda b, j: (j, 0)),
            pl.BlockSpec((1, gm_w), const),
            pl.BlockSpec((1, gm_w), const),
        ],
        out_specs=[o[1] for o in outs],
        out_shape=[o[0] for o in outs],
        compiler_params=pltpu.CompilerParams(dimension_semantics=("parallel", "parallel"),
                                             vmem_limit_bytes=VMEM_LIMIT),
        name="mix_in_prompt",
    )(x3d, g_pre, w_in_bf, cos_t, sin_t, ln_g, ln_b)


def _mix_in_sample_body(x_ref, g_ref, w_ref, cos_ref, sin_ref, lng_ref, lnb_ref,
                        q_ref, k_ref, v_ref, u_ref, vn_ref, **widths):
    q, k, v, u, vn = _mix_in_core(x_ref[...], g_ref, w_ref, cos_ref, sin_ref, lng_ref, lnb_ref, **widths)
    q_ref[...] = q.astype(BF16)
    k_ref[...] = k
    v_ref[...] = v
    u_ref[...] = u
    vn_ref[...] = vn


def _mix_in_sample(x2d, g_pre, w_in_bf, cos_t, sin_t, ln_g, ln_b, *, n_t, qk_w, attn_w, gm_w):
    n_dec = x2d.shape[0]
    d_model = w_in_bf.shape[0]
    const = lambda t: (0, 0)
    colblk = lambda t: (0, t)

    def out(width, dtype):
        return jax.ShapeDtypeStruct((n_dec, n_t * width), dtype), pl.BlockSpec((n_dec, width), colblk)

    outs = [out(qk_w, BF16), out(qk_w, F32), out(attn_w, F32), out(gm_w, F32), out(gm_w, F32)]
    return pl.pallas_call(
        functools.partial(_mix_in_sample_body, qk_w=qk_w, attn_w=attn_w, gm_w=gm_w, q_scale=HEAD_DIM ** -0.5),
        grid=(n_t,),
        in_specs=[
            pl.BlockSpec((n_dec, d_model), colblk),
            pl.BlockSpec((1, d_model), const),
            pl.BlockSpec(w_in_bf.shape, const),
            pl.BlockSpec((n_dec, LANES), lambda t: (t, 0)),
            pl.BlockSpec((n_dec, LANES), lambda t: (t, 0)),
            pl.BlockSpec((1, gm_w), const),
            pl.BlockSpec((1, gm_w), const),
        ],
        out_specs=[o[1] for o in outs],
        out_shape=[o[0] for o in outs],
        compiler_params=pltpu.CompilerParams(dimension_semantics=("parallel",),
                                             vmem_limit_bytes=VMEM_LIMIT),
        name="mix_in_sample",
    )(x2d, g_pre, w_in_bf, cos_t, sin_t, ln_g, ln_b)


def _prompt_attn_body(qi_ref, kj_ref, qt_ref, k_ref, vt_ref, lq1_ref, lk1_ref, lq2_ref, lk2_ref, sg_ref,
                      o_ref, m_sc, l_sc, a_sc, *, lam_init, q_slab):
    p = pl.program_id(2)
    qi = qi_ref[p]
    kj = kj_ref[p]
    tq = qt_ref.shape[2]
    tk = k_ref.shape[1]
    units = [(s, c) for s in range(2) for c in range(tq // q_slab)]

    @pl.when(kj == 0)
    def _():
        m_sc[...] = jnp.full(m_sc.shape, NEG_INF, F32)
        l_sc[...] = jnp.zeros(l_sc.shape, F32)
        a_sc[...] = jnp.zeros(a_sc.shape, F32)

    def block(diagonal):
        n_keys = (lambda c: (c + 1) * q_slab) if diagonal else (lambda c: tk)

        def scores(s, c):
            qs = qt_ref[0, :, c * q_slab:(c + 1) * q_slab]
            zeros = jnp.zeros((HEAD_DIM, q_slab), BF16)
            q_half = (jnp.concatenate([qs[:HEAD_DIM], zeros], axis=0) if s == 0
                      else jnp.concatenate([zeros, qs[HEAD_DIM:]], axis=0))
            return _dot(k_ref[0, :n_keys(c), :], q_half)

        def accumulate(st, s, c):
            cols = slice(c * q_slab, (c + 1) * q_slab)
            if diagonal:
                key = lax.broadcasted_iota(jnp.int32, st.shape, 0)
                qry = lax.broadcasted_iota(jnp.int32, st.shape, 1) + c * q_slab
                st = jnp.where(key <= qry, st, NEG_INF)
            m_prev = m_sc[s, :, cols]
            m_new = jnp.maximum(m_prev, jnp.max(st, axis=0, keepdims=True))
            alpha = jnp.exp2(m_prev - m_new)
            pt = jnp.exp2(st - m_new)
            l_sc[s, :, cols] = alpha * l_sc[s, :, cols] + jnp.sum(pt, axis=0, keepdims=True)
            a_sc[s, :, cols] = alpha * a_sc[s, :, cols] + _dot(vt_ref[0, :, :n_keys(c)], pt.astype(BF16))
            m_sc[s, :, cols] = m_new

        st_next = scores(*units[0])
        for u, (s, c) in enumerate(units):
            st = st_next
            if u + 1 < len(units):
                st_next = scores(*units[u + 1])
            accumulate(st, s, c)

    @pl.when(kj < qi)
    def _():
        block(False)

    @pl.when(kj == qi)
    def _():
        block(True)
        lam = _diff_lambda(lq1_ref[...], lk1_ref[...], lq2_ref[...], lk2_ref[...], lam_init)
        ot = a_sc[0] / l_sc[0] - lam * (a_sc[1] / l_sc[1])
        ms = jnp.mean(ot * ot, axis=0, keepdims=True)
        y = ot * lax.rsqrt(ms + NORM_EPS) * sg_ref[...] * (1.0 - lam_init)
        o_ref[0] = y.T.astype(o_ref.dtype)


def _prompt_attn(qt, k, vt, lq1, lk1, lq2, lk2, subln_g_col, *, lam_init, n_heads, tq):
    B, S, _ = k.shape
    nq = S // tq
    pairs = [(i, j) for i in range(nq) for j in range(i + 1)]
    qi = jnp.asarray([p[0] for p in pairs], jnp.int32)
    kj = jnp.asarray([p[1] for p in pairs], jnp.int32)
    const = lambda b, h, p, qi, kj: (0, 0)
    grid_spec = pltpu.PrefetchScalarGridSpec(
        num_scalar_prefetch=2,
        grid=(B, n_heads, len(pairs)),
        in_specs=[
            pl.BlockSpec((1, V_DIM, tq), lambda b, h, p, qi, kj: (b, h, qi[p])),
            pl.BlockSpec((1, tq, V_DIM), lambda b, h, p, qi, kj: (b, kj[p], h)),
            pl.BlockSpec((1, V_DIM, tq), lambda b, h, p, qi, kj: (b, h, kj[p])),
            pl.BlockSpec((1, HEAD_DIM), const),
            pl.BlockSpec((1, HEAD_DIM), const),
            pl.BlockSpec((1, HEAD_DIM), const),
            pl.BlockSpec((1, HEAD_DIM), const),
            pl.BlockSpec((V_DIM, 1), const),
        ],
        out_specs=pl.BlockSpec((1, tq, V_DIM), lambda b, h, p, qi, kj: (b, qi[p], h)),
        scratch_shapes=[pltpu.VMEM((2, 1, tq), F32), pltpu.VMEM((2, 1, tq), F32), pltpu.VMEM((2, V_DIM, tq), F32)],
    )
    return pl.pallas_call(
        functools.partial(_prompt_attn_body, lam_init=lam_init, q_slab=ATTN_Q_SLAB),
        grid_spec=grid_spec,
        out_shape=jax.ShapeDtypeStruct((B, S, n_heads * V_DIM), BF16),
        compiler_params=pltpu.CompilerParams(
            dimension_semantics=("parallel", "parallel", "arbitrary"), vmem_limit_bytes=VMEM_LIMIT),
        name="prompt_attn",
    )(qi, kj, qt, k, vt, lq1, lk1, lq2, lk2, subln_g_col)


def _paged_init(m_sc, l_sc, acc_sc):
    m_sc[...] = jnp.full(m_sc.shape, NEG_INF, F32)
    l_sc[...] = jnp.zeros(l_sc.shape, F32)
    acc_sc[...] = jnp.zeros(acc_sc.shape, F32)


def _paged_scores(qbd, k_refs):
    pairs = [jnp.concatenate([r[...].astype(BF16) for r in k_refs[i:i + 2]], axis=1) for i in range(0, len(k_refs), 2)]
    return jnp.concatenate([_dot(qbd, kk) for kk in pairs], axis=1)


def _paged_accumulate(s, t, v_refs, m_sc, l_sc, acc_sc, *, n_heads):
    page = v_refs[0].shape[0] // n_heads
    m_prev = m_sc[t]
    m_new = jnp.maximum(m_prev, jnp.max(s, axis=-1, keepdims=True))
    alpha = jnp.exp(m_prev - m_new)
    pr = jnp.exp(s - m_new)
    l_sc[t] = alpha * l_sc[t] + jnp.sum(pr, axis=-1, keepdims=True)
    prb = pr.astype(BF16)

    def head(h):
        return jnp.concatenate([r[pl.ds(h, page, stride=n_heads), :] for r in v_refs], axis=0).astype(BF16)

    for h in range(0, n_heads, 2):
        cols = slice(h * V_DIM, (h + 2) * V_DIM)
        acc_sc[t, :, cols] = alpha * acc_sc[t, :, cols] + _dot(prb, jnp.concatenate([head(h), head(h + 1)], axis=1))
    m_sc[t] = m_new


def _paged_finish(qbd, kn_ref, vn_ref, lam, sg_ref, o_ref, m_sc, l_sc, acc_sc, y_sc, *, lam_init, n_heads):
    n_streams, n_rows, _ = acc_sc.shape
    n_new = kn_ref.shape[1]
    half = n_rows // 2
    qf = qbd.astype(F32)
    q_tok = lax.broadcasted_iota(jnp.int32, (n_rows, 1), 0) % n_new
    s_new = []
    for j in range(n_new):
        kj = kn_ref[0, j:j + 1, :].astype(BF16).astype(F32)
        s_new.append(jnp.where(q_tok >= j, jnp.sum(qf * kj, axis=-1, keepdims=True), NEG_INF))
    m_fin = m_sc[0]
    for t in range(1, n_streams):
        m_fin = jnp.maximum(m_fin, m_sc[t])
    for sj in s_new:
        m_fin = jnp.maximum(m_fin, sj)
    l_fin = jnp.zeros_like(m_fin)
    acc = jnp.zeros(acc_sc.shape[1:], F32)
    for t in range(n_streams):
        alpha = jnp.exp(m_sc[t] - m_fin)
        l_fin = l_fin + alpha * l_sc[t]
        acc = acc + alpha * acc_sc[t]
    for j, sj in enumerate(s_new):
        pj = jnp.exp(sj - m_fin)
        l_fin = l_fin + pj
        acc = acc + pj.astype(BF16).astype(F32) * vn_ref[0, j:j + 1, :].astype(BF16).astype(F32)
    o = acc / l_fin
    o = o[:half] - lam * o[half:]
    for h in range(n_heads):
        blk = o[:, h * V_DIM:(h + 1) * V_DIM]
        y_sc[:, h * V_DIM:(h + 1) * V_DIM] = _rms(blk, sg_ref[...]) * (1.0 - lam_init)
    for h in range(n_heads):
        o_ref[0, :, h * V_DIM:(h + 1) * V_DIM] = y_sc[h * n_new:(h + 1) * n_new, h * V_DIM:(h + 1) * V_DIM]


def _mix_out_prompt_body(x_ref, o_ref, u_ref, vn_ref, ws_ref, bs_ref, wout_ref, gpost_ref, y_ref, gate_sc):
    tm = x_ref.shape[0]
    attn_w = o_ref.shape[1]
    gd = vn_ref.shape[1] // N_GROUPS
    row = lax.broadcasted_iota(jnp.int32, (CHUNK, CHUNK), 0)
    col = lax.broadcasted_iota(jnp.int32, (CHUNK, CHUNK), 1)
    for g in range(N_GROUPS):
        wg = jnp.where(col <= row, ws_ref[g], 0.0).astype(BF16)
        bias = bs_ref[:, g * gd:(g + 1) * gd]
        for c in range(tm // CHUNK):
            rows = slice(c * CHUNK, (c + 1) * CHUNK)
            cols = slice(g * gd, (g + 1) * gd)
            s = _dot(wg, vn_ref[rows, cols]) + bias
            gate_sc[rows, cols] = (u_ref[rows, cols] * s).astype(BF16)
    m = _dot(o_ref[...], wout_ref[:attn_w, :]) + _dot(gate_sc[...], wout_ref[attn_w:, :])
    y_ref[...] = x_ref[...] + _rms(m, gpost_ref[...])


def _mix_out_prompt(x2d, o2d, u2d, vn2d, w_spatial, bs_exp, w_out_bf, g_post, *, tm):
    n_rows, d_model = x2d.shape
    attn_w = o2d.shape[1]
    gm_w = u2d.shape[1]
    const2 = lambda i: (0, 0)
    rows = lambda i: (i, 0)
    return pl.pallas_call(
        _mix_out_prompt_body,
        grid=(n_rows // tm,),
        in_specs=[
            pl.BlockSpec((tm, d_model), rows),
            pl.BlockSpec((tm, attn_w), rows),
            pl.BlockSpec((tm, gm_w), rows),
            pl.BlockSpec((tm, gm_w), rows),
            pl.BlockSpec(w_spatial.shape, lambda i: (0, 0, 0)),
            pl.BlockSpec(bs_exp.shape, const2),
            pl.BlockSpec(w_out_bf.shape, const2),
            pl.BlockSpec((1, d_model), const2),
        ],
        out_specs=pl.BlockSpec((tm, d_model), rows),
        out_shape=jax.ShapeDtypeStruct((n_rows, d_model), F32),
        scratch_shapes=[pltpu.VMEM((tm, gm_w), BF16)],
        compiler_params=pltpu.CompilerParams(dimension_semantics=("parallel",),
                                             vmem_limit_bytes=VMEM_LIMIT),
        name="mix_out_prompt",
    )(x2d, o2d, u2d, vn2d, w_spatial, bs_exp, w_out_bf, g_post)


def _mix_out_sample_body(x_ref, o_ref, u_ref, vn_ref, coef_ref, bias_ref, wout_ref, gpost_ref, y_ref, hist):
    t = pl.program_id(0)
    n_t = hist.shape[0]
    attn_w = o_ref.shape[1]

    @pl.when(t == 0)
    def _():
        hist[...] = jnp.zeros(hist.shape, F32)

    hist[t] = vn_ref[...]
    s = bias_ref[pl.ds(t, 1), :]
    for j in range(n_t):
        cj = jnp.where(j <= t, coef_ref[pl.ds(t * n_t + j, 1), :], 0.0)
        s = s + cj.astype(BF16).astype(F32) * hist[j].astype(BF16).astype(F32)
    gate = (u_ref[...] * s).astype(BF16)
    m = _dot(o_ref[...].astype(BF16), wout_ref[:attn_w, :]) + _dot(gate, wout_ref[attn_w:, :])
    y_ref[...] = x_ref[...] + _rms(m, gpost_ref[...])


def _mix_out_sample(x2d, o2d, u2d, vn2d, coef, bias, w_out_bf, g_post, *, n_t):
    n_dec = x2d.shape[0]
    d_model = x2d.shape[1] // n_t
    attn_w = o2d.shape[1] // n_t
    gm_w = u2d.shape[1] // n_t
    const2 = lambda t: (0, 0)
    colblk = lambda t: (0, t)
    return pl.pallas_call(
        _mix_out_sample_body,
        grid=(n_t,),
        in_specs=[
            pl.BlockSpec((n_dec, d_model), colblk),
            pl.BlockSpec((n_dec, attn_w), colblk),
            pl.BlockSpec((n_dec, gm_w), colblk),
            pl.BlockSpec((n_dec, gm_w), colblk),
            pl.BlockSpec(coef.shape, const2),
            pl.BlockSpec(bias.shape, const2),
            pl.BlockSpec(w_out_bf.shape, const2),
            pl.BlockSpec((1, d_model), const2),
        ],
        out_specs=pl.BlockSpec((n_dec, d_model), colblk),
        out_shape=jax.ShapeDtypeStruct(x2d.shape, F32),
        scratch_shapes=[pltpu.VMEM((n_t, n_dec, gm_w), F32)],
        compiler_params=pltpu.CompilerParams(dimension_semantics=("arbitrary",),
                                             vmem_limit_bytes=VMEM_LIMIT),
        name="mix_out_sample",
    )(x2d, o2d, u2d, vn2d, coef, bias, w_out_bf, g_post)


def _ffn_chunks(h, taps, cw_ref, cb_ref, wup_ref, wdn_ref, *, d_ff, col_chunk, chunk_ids, between=()):
    gate_cols = lambda c: slice(c * col_chunk, (c + 1) * col_chunk)
    val_cols = lambda c: slice(d_ff + c * col_chunk, d_ff + (c + 1) * col_chunk)

    def up(c):
        return _dot(h, wup_ref[:, gate_cols(c)]), _dot(h, wup_ref[:, val_cols(c)])

    def conv(u, cols):
        shifted = taps(u, cols)
        y = cb_ref[:, cols] + cw_ref[0:1, cols] * shifted[0]
        for j in range(1, CONV_W):
            y = y + cw_ref[j:j + 1, cols] * shifted[j]
        return y

    def down(a, c):
        return _dot(a, wdn_ref[gate_cols(c), :])

    acc = None
    act = None
    up_next = up(chunk_ids[0])
    for i, c in enumerate(chunk_ids):
        up_gate, up_val = up_next
        if i + 1 < len(chunk_ids):
            up_next = up(chunk_ids[i + 1])
        if act is not None:
            part = down(act, chunk_ids[i - 1])
            acc = part if acc is None else acc + part
        act = (_gelu(conv(up_gate, gate_cols(c))) * conv(up_val, val_cols(c))).astype(BF16)
        if i < len(between):
            between[i]()
    part = down(act, chunk_ids[-1])
    return part if acc is None else acc + part


def _ffn_prompt_body(pt_ref, x_ref, g_ref, wup_ref, cw_ref, cb_ref, wdn_ref, gpost_ref,
                     qbd_ref, kn_ref, vn_ref, lq1_ref, lk1_ref, lq2_ref, lk2_ref, sg_ref, kt_hbm, v_hbm,
                     y_ref, conv_ref, o_ref,
                     carry1, carry2, h_sc, acc_ffn, m_sc, l_sc, acc_sc, y_sc, kbuf, vbuf, sem, *,
                     d_ff, col_chunk, lam_init, n_pages_step, n_streams, n_heads, chunk_groups):
    b = pl.program_id(0)
    j = pl.program_id(1)
    sub = pl.program_id(2)
    n_sub = len(chunk_groups)
    step = (b * pl.num_programs(1) + j) * n_sub + sub
    last_step = pl.num_programs(0) * pl.num_programs(1) * n_sub - 1
    slot = step % 2
    per = n_pages_step // n_streams
    row = lax.broadcasted_iota(jnp.int32, (SUBLANES, 1), 0)

    def page_copies(of_step, into_slot):
        seq = of_step // n_sub
        first_page = (of_step % n_sub) * n_pages_step
        copies = []
        for p in range(n_pages_step):
            phys = pt_ref[seq, first_page + p]
            copies.append(pltpu.make_async_copy(kt_hbm.at[phys], kbuf.at[into_slot, p], sem.at[into_slot, 0]))
            copies.append(pltpu.make_async_copy(v_hbm.at[phys], vbuf.at[into_slot, p], sem.at[into_slot, 1]))
        return copies

    @pl.when(step == 0)
    def _():
        for c in page_copies(step, slot):
            c.start()

    k_refs = [kbuf.at[slot, p] for p in range(n_pages_step)]
    v_refs = [vbuf.at[slot, p] for p in range(n_pages_step)]

    @pl.when((j == 0) & (sub == 0))
    def _():
        carry1[...] = jnp.zeros(carry1.shape, F32)
        carry2[...] = jnp.zeros(carry2.shape, F32)

    def taps(up, cols):
        r1 = pltpu.roll(up, 1, 0)
        r2 = pltpu.roll(up, 2, 0)
        t1 = jnp.concatenate([jnp.where(row < 1, carry1[:, cols], r1[:SUBLANES]), r1[SUBLANES:]], axis=0)
        t2 = jnp.concatenate([jnp.where(row < 2, carry2[:, cols], r2[:SUBLANES]), r2[SUBLANES:]], axis=0)
        carry1[:, cols] = r1[:SUBLANES]
        carry2[:, cols] = r2[:SUBLANES]
        conv_ref[0, :, cols] = r2[:CONV_W - 1]
        return [t2, t1, up]

    def sub_step(grp, chunk_ids):
        first, last = grp == 0, grp == len(chunk_groups) - 1
        for c in page_copies(jnp.minimum(step + 1, last_step), 1 - slot):
            c.start()
        qbd = qbd_ref[0]
        if first:
            h = _rms(x_ref[0], g_ref[...]).astype(BF16)
            h_sc[...] = h
            _paged_init(m_sc, l_sc, acc_sc)
        else:
            h = h_sc[...]
        for c in page_copies(step, slot):
            c.wait()
        scores = [None] * n_streams
        scores[0] = _paged_scores(qbd, k_refs[:per])

        def stream(t):
            def run():
                _paged_accumulate(scores[t], t, v_refs[t * per:(t + 1) * per], m_sc, l_sc, acc_sc, n_heads=n_heads)
                if t + 1 < n_streams:
                    scores[t + 1] = _paged_scores(qbd, k_refs[(t + 1) * per:(t + 2) * per])
            return run

        n_between = min(n_streams, len(chunk_ids))
        part = _ffn_chunks(h, taps, cw_ref, cb_ref, wup_ref, wdn_ref, d_ff=d_ff, col_chunk=col_chunk,
                           chunk_ids=chunk_ids, between=[stream(t) for t in range(n_between)])
        for t in range(n_between, n_streams):
            stream(t)()
        if first:
            acc_ffn[...] = part
        elif not last:
            acc_ffn[...] = acc_ffn[...] + part
        if last:
            total = part if first else acc_ffn[...] + part
            y_ref[0] = x_ref[0] + _rms(total, gpost_ref[...])
            lam = _diff_lambda(lq1_ref[...], lk1_ref[...], lq2_ref[...], lk2_ref[...], lam_init)
            _paged_finish(qbd, kn_ref, vn_ref, lam, sg_ref, o_ref, m_sc, l_sc, acc_sc, y_sc,
                          lam_init=lam_init, n_heads=n_heads)

    for grp, chunk_ids in enumerate(chunk_groups):
        pl.when(sub == grp)(functools.partial(sub_step, grp, chunk_ids))

    @pl.when(step == last_step)
    def _():
        for c in page_copies(step, 1 - slot):
            c.wait()


def _ffn_prompt_paged(x3d, g_pre, w_up_bf, conv_w, conv_b, w_down_bf, g_post,
                      page_table, qbd, k_new, v_new, cache_kt, cache_v, lq1, lk1, lq2, lk2, subln_g, *,
                      tm, col_chunk, lam_init, n_heads, n_pages_step, n_streams):
    B, S, d_model = x3d.shape
    d_ff = w_down_bf.shape[0]
    n_dec, n_pages = page_table.shape
    _, width, page = cache_kt.shape
    n_new = k_new.shape[1]
    n_rows = qbd.shape[1]
    G = n_pages_step
    tiles = S // tm
    n_sub = n_pages // G
    n_chunks = d_ff // col_chunk
    assert B * tiles == n_dec and n_pages % G == 0 and G % (2 * n_streams) == 0 and n_sub <= n_chunks
    assert n_heads % 2 == 0
    bounds = [round(i * n_chunks / n_sub) for i in range(n_sub + 1)]
    chunk_groups = tuple(tuple(range(bounds[i], bounds[i + 1])) for i in range(n_sub))

    const2 = lambda b, j, s, pt: (0, 0)
    tile = lambda b, j, s, pt: (b, j, 0)
    per_seq = lambda b, j, s, pt: (b * tiles + j, 0, 0)

    grid_spec = pltpu.PrefetchScalarGridSpec(
        num_scalar_prefetch=1,
        grid=(B, tiles, n_sub),
        in_specs=[
            pl.BlockSpec((1, tm, d_model), tile),
            pl.BlockSpec((1, d_model), const2),
            pl.BlockSpec(w_up_bf.shape, const2, pipeline_mode=pl.Buffered(1)),
            pl.BlockSpec(conv_w.shape, const2),
            pl.BlockSpec(conv_b.shape, const2),
            pl.BlockSpec(w_down_bf.shape, const2, pipeline_mode=pl.Buffered(1)),
            pl.BlockSpec((1, d_model), const2),
            pl.BlockSpec((1, n_rows, width), per_seq),
            pl.BlockSpec((1, n_new, width), per_seq),
            pl.BlockSpec((1, n_new, width), per_seq),
            pl.BlockSpec((1, HEAD_DIM), const2),
            pl.BlockSpec((1, HEAD_DIM), const2),
            pl.BlockSpec((1, HEAD_DIM), const2),
            pl.BlockSpec((1, HEAD_DIM), const2),
            pl.BlockSpec((1, V_DIM), const2),
            pl.BlockSpec(memory_space=pl.ANY),
            pl.BlockSpec(memory_space=pl.ANY),
        ],
        out_specs=[
            pl.BlockSpec((1, tm, d_model), tile),
            pl.BlockSpec((1, CONV_W - 1, 2 * d_ff), lambda b, j, s, pt: (b, 0, 0)),
            pl.BlockSpec((1, n_new, width), per_seq),
        ],
        scratch_shapes=[
            pltpu.VMEM((SUBLANES, 2 * d_ff), F32), pltpu.VMEM((SUBLANES, 2 * d_ff), F32),
            pltpu.VMEM((tm, d_model), BF16), pltpu.VMEM((tm, d_model), F32),
            pltpu.VMEM((n_streams, n_rows, 1), F32), pltpu.VMEM((n_streams, n_rows, 1), F32),
            pltpu.VMEM((n_streams, n_rows, width), F32), pltpu.VMEM((n_rows // 2, width), F32),
            pltpu.VMEM((2, G, width, page), F32), pltpu.VMEM((2, G, page * n_heads, V_DIM), F32),
            pltpu.SemaphoreType.DMA((2, 2)),
        ],
    )
    return pl.pallas_call(
        functools.partial(_ffn_prompt_body, d_ff=d_ff, col_chunk=col_chunk, lam_init=lam_init, n_pages_step=G,
                          n_streams=n_streams, n_heads=n_heads, chunk_groups=chunk_groups),
        grid_spec=grid_spec,
        out_shape=[
            jax.ShapeDtypeStruct((B, S, d_model), F32),
            jax.ShapeDtypeStruct((B, CONV_W - 1, 2 * d_ff), F32),
            jax.ShapeDtypeStruct((n_dec, n_new, width), F32),
        ],
        compiler_params=pltpu.CompilerParams(dimension_semantics=("arbitrary", "arbitrary", "arbitrary"),
                                             vmem_limit_bytes=VMEM_LIMIT),
        name="ffn_prompt_paged",
    )(page_table, x3d, g_pre, w_up_bf, conv_w, conv_b, w_down_bf, g_post,
      qbd, k_new, v_new, lq1, lk1, lq2, lk2, subln_g, cache_kt, cache_v)


def _ffn_sample_body(x_ref, st_ref, g_ref, wup_ref, cw_ref, cb_ref, wdn_ref, gpost_ref, y_ref, conv_ref, ring, *,
                     d_ff, col_chunk):
    t = pl.program_id(0)
    width = 2 * d_ff

    @pl.when(t == 0)
    def _():
        for j in range(CONV_W - 1):
            ring[j] = st_ref[:, j * width:(j + 1) * width]

    x = x_ref[...]
    h = _rms(x, g_ref[...]).astype(BF16)
    cur = (t + CONV_W - 1) % CONV_W

    def taps(up, cols):
        ring[cur, :, cols] = up
        conv_ref[:, cols] = up
        return [ring[(t + j) % CONV_W, :, cols] for j in range(CONV_W - 1)] + [up]

    acc = _ffn_chunks(h, taps, cw_ref, cb_ref, wup_ref, wdn_ref, d_ff=d_ff, col_chunk=col_chunk,
                      chunk_ids=tuple(range(d_ff // col_chunk)))
    y_ref[...] = x + _rms(acc, gpost_ref[...])


def _ffn_sample(x2d, state2d, g_pre, w_up_bf, conv_w, conv_b, w_down_bf, g_post, *, n_t, col_chunk):
    n_dec = x2d.shape[0]
    d_model = x2d.shape[1] // n_t
    d_ff = w_down_bf.shape[0]
    width = 2 * d_ff
    const2 = lambda t: (0, 0)
    colblk = lambda t: (0, t)
    conv_map = lambda t: (0, jnp.maximum(t - (n_t - (CONV_W - 1)), 0))
    return pl.pallas_call(
        functools.partial(_ffn_sample_body, d_ff=d_ff, col_chunk=col_chunk),
        grid=(n_t,),
        in_specs=[
            pl.BlockSpec((n_dec, d_model), colblk),
            pl.BlockSpec(state2d.shape, const2),
            pl.BlockSpec((1, d_model), const2),
            pl.BlockSpec(w_up_bf.shape, const2, pipeline_mode=pl.Buffered(1)),
            pl.BlockSpec(conv_w.shape, const2),
            pl.BlockSpec(conv_b.shape, const2),
            pl.BlockSpec(w_down_bf.shape, const2, pipeline_mode=pl.Buffered(1)),
            pl.BlockSpec((1, d_model), const2),
        ],
        out_specs=[
            pl.BlockSpec((n_dec, d_model), colblk),
            pl.BlockSpec((n_dec, width), conv_map),
        ],
        out_shape=[
            jax.ShapeDtypeStruct(x2d.shape, F32),
            jax.ShapeDtypeStruct((n_dec, (CONV_W - 1) * width), F32),
        ],
        scratch_shapes=[pltpu.VMEM((CONV_W, n_dec, width), F32)],
        compiler_params=pltpu.CompilerParams(dimension_semantics=("arbitrary",),
                                             vmem_limit_bytes=VMEM_LIMIT),
        name="ffn_sample",
    )(x2d, state2d, g_pre, w_up_bf, conv_w, conv_b, w_down_bf, g_post)


def _rope_tables(pos):
    half = HEAD_DIM // 2
    inv = ROPE_THETA ** (-jnp.arange(half, dtype=F32) * 2.0 / HEAD_DIM)
    ang = pos[:, None] * inv[None, :]
    cos = jnp.cos(ang)
    sin = jnp.sin(ang)
    cos_t = jnp.concatenate([cos, cos, cos, cos], axis=1)
    sin_t = jnp.concatenate([-sin, sin, -sin, sin], axis=1)
    return cos_t, sin_t


def kernel(x_prompt, x_sample, cache_k, cache_v, state_conv, page_table, norm_mix_pre, w_in, lambda_q1,
           lambda_k1, lambda_q2, lambda_k2, subln_g, gate_ln_g, gate_ln_b, w_spatial, b_spatial, w_out,
           norm_mix_post, norm_ffn_pre, w_up, conv_w, conv_b, w_down, norm_ffn_post):
    B, S, d_model = x_prompt.shape
    n_dec, n_t, _ = x_sample.shape
    depth, n_phys, page, n_heads, _, _ = cache_k.shape
    qk_w = n_heads * 2 * HEAD_DIM
    attn_w = n_heads * V_DIM
    gm_w = (w_in.shape[2] - 2 * qk_w - attn_w) // 2
    d_ff = w_down.shape[1]
    past_len = page_table.shape[1] * page
    assert depth == 1 and w_spatial.shape[1] == N_GROUPS and w_spatial.shape[2] == CHUNK
    assert CONV_W - 1 <= n_t <= CHUNK and conv_w.shape[1] == CONV_W
    widths = dict(qk_w=qk_w, attn_w=attn_w, gm_w=gm_w)

    l = 0
    lam_init = 0.8 - 0.6 * math.exp(-0.3 * l)
    row = lambda a: a[l].reshape(1, -1)
    w_in_bf = w_in[l].astype(BF16)
    w_out_bf = w_out[l].astype(BF16)
    w_up_bf = w_up[l].astype(BF16)
    w_down_bf = w_down[l].astype(BF16)
    lq1, lk1, lq2, lk2 = row(lambda_q1), row(lambda_k1), row(lambda_q2), row(lambda_k2)
    sg = row(subln_g)
    g_mix_pre, g_mix_post = row(norm_mix_pre), row(norm_mix_post)
    g_ffn_pre, g_ffn_post = row(norm_ffn_pre), row(norm_ffn_post)
    ln_g, ln_b = row(gate_ln_g), row(gate_ln_b)
    conv_b2 = row(conv_b)
    gd = gm_w // N_GROUPS

    cos_p, sin_p = _rope_tables(jnp.arange(S, dtype=F32))
    qt_p, kt_p, kb_p, vf_p, vt_p, u_p, vn_p = _mix_in_prompt(
        x_prompt, g_mix_pre, w_in_bf, cos_p, sin_p, ln_g, ln_b, tm=TILE_MIX_IN, **widths)
    o_p = _prompt_attn(qt_p, kb_p, vt_p, lq1, lk1, lq2, lk2, sg.reshape(V_DIM, 1),
                       lam_init=lam_init, n_heads=n_heads, tq=TILE_ATTN)
    bs_exp = jnp.repeat(b_spatial[l].T, gd, axis=1)
    x1_p = _mix_out_prompt(x_prompt.reshape(B * S, d_model), o_p.reshape(B * S, attn_w),
                           u_p.reshape(B * S, gm_w), vn_p.reshape(B * S, gm_w), w_spatial[l], bs_exp, w_out_bf,
                           g_mix_post, tm=TILE_MIX_OUT)
    new_k_p = kt_p.reshape(B, n_heads, 2, HEAD_DIM, S).transpose(0, 4, 1, 2, 3)[None]

    cos_s, sin_s = _rope_tables(jnp.arange(n_t, dtype=F32) + float(past_len))
    cos_s = jnp.repeat(cos_s, n_dec, axis=0)
    sin_s = jnp.repeat(sin_s, n_dec, axis=0)
    xs2d = x_sample.reshape(n_dec, n_t * d_model)
    q_s, kf_s, vf_s, u_s, vn_s = _mix_in_sample(xs2d, g_mix_pre, w_in_bf, cos_s, sin_s, ln_g, ln_b, n_t=n_t,
                                                **widths)
    q_hs = q_s.reshape(n_dec, n_t, n_heads, 2, HEAD_DIM).transpose(0, 3, 2, 1, 4)
    same = ((jnp.arange(2)[:, None, None, None] == jnp.arange(2)[None, None, None, :])
            & (jnp.arange(n_heads)[None, :, None, None] == jnp.arange(n_heads)[None, None, :, None]))
    qbd = jnp.where(same[None, :, :, None, :, :, None], q_hs[:, :, :, :, None, None, :], jnp.zeros((), BF16))
    qbd = qbd.reshape(n_dec, 2 * n_heads * n_t, qk_w)
    cache_kt = cache_k[l].transpose(0, 2, 3, 4, 1).reshape(n_phys, qk_w, page)
    cache_v2 = cache_v[l].reshape(n_phys, page * n_heads, V_DIM)

    y_p, conv_p, o_s = _ffn_prompt_paged(
        x1_p.reshape(B, S, d_model), g_ffn_pre, w_up_bf, conv_w[l], conv_b2, w_down_bf, g_ffn_post,
        page_table, qbd, kf_s.reshape(n_dec, n_t, qk_w), vf_s.reshape(n_dec, n_t, attn_w), cache_kt, cache_v2,
        lq1, lk1, lq2, lk2, sg, tm=TILE_FFN, col_chunk=FFN_COL_CHUNK, lam_init=lam_init, n_heads=n_heads,
        n_pages_step=PAGES_PER_STEP, n_streams=PAGE_STREAMS)

    coef = jnp.repeat(w_spatial[l][:, :n_t, :n_t].transpose(1, 2, 0).reshape(n_t * n_t, N_GROUPS), gd, axis=1)
    bias = jnp.repeat(b_spatial[l][:, :n_t].T, gd, axis=1)
    x1_s = _mix_out_sample(xs2d, o_s.reshape(n_dec, n_t * attn_w), u_s, vn_s, coef, bias, w_out_bf, g_mix_post,
                           n_t=n_t)
    y_s, conv_s = _ffn_sample(x1_s, state_conv[l].reshape(n_dec, (CONV_W - 1) * 2 * d_ff), g_ffn_pre, w_up_bf,
                              conv_w[l], conv_b2, w_down_bf, g_ffn_post, n_t=n_t, col_chunk=FFN_COL_CHUNK)

    return (y_p,
            y_s.reshape(n_dec, n_t, d_model),
            new_k_p,
            vf_p.reshape(1, B, S, n_heads, V_DIM),
            conv_p.reshape(1, B, CONV_W - 1, 2 * d_ff),
            kf_s.reshape(1, n_dec, n_t, n_heads, 2, HEAD_DIM),
            vf_s.reshape(1, n_dec, n_t, n_heads, V_DIM),
            conv_s.reshape(1, n_dec, CONV_W - 1, 2 * d_ff),
            vn_s.reshape(1, n_dec, n_t, gm_w))
```

```python
import functools
import math

import jax
import jax.numpy as jnp
from jax import lax
from jax.experimental import pallas as pl
from jax.experimental.pallas import tpu as pltpu

F32 = jnp.float32
BF16 = jnp.bfloat16

HEAD_DIM = 64
V_DIM = 2 * HEAD_DIM
CHUNK = 128
N_GROUPS = 4
CONV_W = 3
ROPE_THETA = 10000.0
NORM_EPS = 1e-6
NEG_INF = -1e30
LANES = 128
SUBLANES = 8
VMEM_LIMIT = 56 * 1024 * 1024

TILE_MIX_IN = 512
TILE_ATTN = 2048
ATTN_Q_SLAB = 256
ATTN_KEY_PIECE = 512
TILE_MIX_OUT = 512
TILE_FFN = 256
FFN_COL_CHUNK = 256
PAGES_PER_STEP = 16
PAGE_STREAMS = 2


def _rms(x, g):
    return x * lax.rsqrt(jnp.mean(x * x, axis=-1, keepdims=True) + NORM_EPS) * g


def _gelu(x):
    c = math.sqrt(2.0 / math.pi)
    hx = 0.5 * x
    return hx + hx * jnp.tanh(x * (c + (c * 0.044715) * (x * x)))


def _dot(a, b):
    return jnp.dot(a, b, preferred_element_type=F32)


def _diff_lambda(lq1, lk1, lq2, lk2, lam_init):
    a = jnp.sum(lq1 * lk1, axis=-1, keepdims=True)
    b = jnp.sum(lq2 * lk2, axis=-1, keepdims=True)
    return jnp.exp(a) - jnp.exp(b) + lam_init


def _rope(z, cos, sin_signed, lo_half):
    outs = []
    for c in range(z.shape[1] // LANES):
        xs = z[:, c * LANES:(c + 1) * LANES]
        ahead = pltpu.roll(xs, LANES - HEAD_DIM // 2, 1)
        behind = pltpu.roll(xs, HEAD_DIM // 2, 1)
        outs.append(xs * cos + jnp.where(lo_half, ahead, behind) * sin_signed)
    return jnp.concatenate(outs, axis=1)


def _mix_in_core(x, g_ref, w_ref, cos_ref, sin_ref, lng_ref, lnb_ref, *, qk_w, attn_w, gm_w, q_scale):
    h = _rms(x, g_ref[...]).astype(BF16)
    cos = cos_ref[...]
    sin_signed = sin_ref[...]
    lane = lax.broadcasted_iota(jnp.int32, (1, LANES), 1)
    lo_half = (lane % HEAD_DIM) < (HEAD_DIM // 2)
    c0 = 0
    q = _rope(_dot(h, w_ref[:, c0:c0 + qk_w]), cos, sin_signed, lo_half) * q_scale
    c0 += qk_w
    k = _rope(_dot(h, w_ref[:, c0:c0 + qk_w]), cos, sin_signed, lo_half)
    c0 += qk_w
    v = _dot(h, w_ref[:, c0:c0 + attn_w])
    c0 += attn_w
    u = _gelu(_dot(h, w_ref[:, c0:c0 + gm_w]))
    c0 += gm_w
    gv = _gelu(_dot(h, w_ref[:, c0:c0 + gm_w]))
    mu = jnp.mean(gv, axis=-1, keepdims=True)
    gc = gv - mu
    vn = gc * lax.rsqrt(jnp.mean(gc * gc, axis=-1, keepdims=True) + NORM_EPS)
    return q, k, v, u, vn * lng_ref[...] + lnb_ref[...]


def _mix_in_prompt_body(x_ref, g_ref, w_ref, cos_ref, sin_ref, lng_ref, lnb_ref,
                        qt_ref, kt_ref, kb_ref, vf_ref, vt_ref, u_ref, vn_ref, **widths):
    q, k, v, u, vn = _mix_in_core(x_ref[0], g_ref, w_ref, cos_ref, sin_ref, lng_ref, lnb_ref, **widths)
    qt_ref[0] = q.T.astype(BF16)
    kt_ref[0] = k.T
    kb_ref[0] = k.astype(BF16)
    tm = v.shape[0]
    n_heads = v.shape[1] // V_DIM
    for h in range(n_heads):
        vf_ref[pl.ds(0, 1), pl.ds(h, tm, stride=n_heads), :] = v[:, h * V_DIM:(h + 1) * V_DIM][None]
    vt_ref[0] = v.T.astype(BF16)
    u_ref[0] = u
    vn_ref[0] = vn.astype(BF16)


def _mix_in_prompt(x3d, g_pre, w_in_bf, cos_t, sin_t, ln_g, ln_b, *, tm, qk_w, attn_w, gm_w):
    B, S, d_model = x3d.shape
    const = lambda b, j: (0, 0)
    rows = lambda b, j: (b, j, 0)
    cols = lambda b, j: (b, 0, j)

    def natural(width, dtype):
        return jax.ShapeDtypeStruct((B, S, width), dtype), pl.BlockSpec((1, tm, width), rows)

    def transposed(width, dtype):
        return jax.ShapeDtypeStruct((B, width, S), dtype), pl.BlockSpec((1, width, tm), cols)

    n_heads = attn_w // V_DIM
    v_rows = (jax.ShapeDtypeStruct((B, S * n_heads, V_DIM), F32), pl.BlockSpec((1, tm * n_heads, V_DIM), rows))
    outs = [transposed(qk_w, BF16), transposed(qk_w, F32), natural(qk_w, BF16), v_rows,
            transposed(attn_w, BF16), natural(gm_w, F32), natural(gm_w, BF16)]
    return pl.pallas_call(
        functools.partial(_mix_in_prompt_body, qk_w=qk_w, attn_w=attn_w, gm_w=gm_w,
                          q_scale=HEAD_DIM ** -0.5 * math.log2(math.e)),
        grid=(B, S // tm),
        in_specs=[
            pl.BlockSpec((1, tm, d_model), rows),
            pl.BlockSpec((1, d_model), const),
            pl.BlockSpec(w_in_bf.shape, const),
            pl.BlockSpec((tm, LANES), lambda b, j: (j, 0)),
            pl.BlockSpec((tm, LANES), lambda b, j: (j, 0)),
            pl.BlockSpec((1, gm_w), const),
            pl.BlockSpec((1, gm_w), const),
        ],
        out_specs=[o[1] for o in outs],
        out_shape=[o[0] for o in outs],
        compiler_params=pltpu.CompilerParams(dimension_semantics=("parallel", "parallel"),
                                             vmem_limit_bytes=VMEM_LIMIT),
        name="mix_in_prompt",
    )(x3d, g_pre, w_in_bf, cos_t, sin_t, ln_g, ln_b)


def _mix_in_sample_body(x_ref, g_ref, w_ref, cos_ref, sin_ref, lng_ref, lnb_ref,
                        q_ref, k_ref, v_ref, u_ref, vn_ref, **widths):
    q, k, v, u, vn = _mix_in_core(x_ref[...], g_ref, w_ref, cos_ref, sin_ref, lng_ref, lnb_ref, **widths)
    q_ref[...] = q.astype(BF16)
    k_ref[...] = k
    v_ref[...] = v
    u_ref[...] = u
    vn_ref[...] = vn


def _mix_in_sample(x2d, g_pre, w_in_bf, cos_t, sin_t, ln_g, ln_b, *, n_t, qk_w, attn_w, gm_w):
    n_dec = x2d.shape[0]
    d_model = w_in_bf.shape[0]
    const = lambda t: (0, 0)
    colblk = lambda t: (0, t)

    def out(width, dtype):
        return jax.ShapeDtypeStruct((n_dec, n_t * width), dtype), pl.BlockSpec((n_dec, width), colblk)

    outs = [out(qk_w, BF16), out(qk_w, F32), out(attn_w, F32), out(gm_w, F32), out(gm_w, F32)]
    return pl.pallas_call(
        functools.partial(_mix_in_sample_body, qk_w=qk_w, attn_w=attn_w, gm_w=gm_w, q_scale=HEAD_DIM ** -0.5),
        grid=(n_t,),
        in_specs=[
            pl.BlockSpec((n_dec, d_model), colblk),
            pl.BlockSpec((1, d_model), const),
            pl.BlockSpec(w_in_bf.shape, const),
            pl.BlockSpec((n_dec, LANES), lambda t: (t, 0)),
            pl.BlockSpec((n_dec, LANES), lambda t: (t, 0)),
            pl.BlockSpec((1, gm_w), const),
            pl.BlockSpec((1, gm_w), const),
        ],
        out_specs=[o[1] for o in outs],
        out_shape=[o[0] for o in outs],
        compiler_params=pltpu.CompilerParams(dimension_semantics=("parallel",),
                                             vmem_limit_bytes=VMEM_LIMIT),
        name="mix_in_sample",
    )(x2d, g_pre, w_in_bf, cos_t, sin_t, ln_g, ln_b)


def _prompt_attn_body(qi_ref, kj_ref, qt_ref, k_ref, vt_ref, lq1_ref, lk1_ref, lq2_ref, lk2_ref, sg_ref,
                      o_ref, m_sc, l_sc, a_sc, *, lam_init, q_slab, key_piece):
    p = pl.program_id(2)
    qi = qi_ref[p]
    kj = kj_ref[p]
    tq = qt_ref.shape[2]
    tk = k_ref.shape[1]
    units = [(s, c) for s in range(2) for c in range(tq // q_slab)]

    @pl.when(kj == 0)
    def _():
        m_sc[...] = jnp.full(m_sc.shape, NEG_INF, F32)
        l_sc[...] = jnp.zeros(l_sc.shape, F32)
        a_sc[...] = jnp.zeros(a_sc.shape, F32)

    def block(diagonal):
        n_keys = (lambda c: (c + 1) * q_slab) if diagonal else (lambda c: tk)

        def key_pieces(c):
            return [(k0, min(k0 + key_piece, n_keys(c))) for k0 in range(0, n_keys(c), key_piece)]

        def score_thunks(s, c):
            qs = qt_ref[0, :, c * q_slab:(c + 1) * q_slab]
            zeros = jnp.zeros((HEAD_DIM, q_slab), BF16)
            q_half = (jnp.concatenate([qs[:HEAD_DIM], zeros], axis=0) if s == 0
                      else jnp.concatenate([zeros, qs[HEAD_DIM:]], axis=0))

            def piece(k0, k1):
                st = _dot(k_ref[0, k0:k1, :], q_half)
                if diagonal and k1 > c * q_slab:
                    key = lax.broadcasted_iota(jnp.int32, st.shape, 0) + k0
                    qry = lax.broadcasted_iota(jnp.int32, st.shape, 1) + c * q_slab
                    st = jnp.where(key <= qry, st, NEG_INF)
                return st
            return [functools.partial(piece, k0, k1) for k0, k1 in key_pieces(c)]

        def accumulate(sts, s, c, next_thunks):
            cols = slice(c * q_slab, (c + 1) * q_slab)
            nxt = []
            take = lambda: nxt.append(next_thunks.pop(0)()) if next_thunks else None
            take()
            m_prev = m_sc[s, :, cols]
            m_new = m_prev
            for st in sts:
                m_new = jnp.maximum(m_new, jnp.max(st, axis=0, keepdims=True))
            alpha = jnp.exp2(m_prev - m_new)
            l_part = None
            a_part = None
            for st, (k0, k1) in zip(sts, key_pieces(c)):
                take()
                pt = jnp.exp2(st - m_new)
                lsum = jnp.sum(pt, axis=0, keepdims=True)
                pv = _dot(vt_ref[0, :, k0:k1], pt.astype(BF16))
                l_part = lsum if l_part is None else l_part + lsum
                a_part = pv if a_part is None else a_part + pv
            while next_thunks:
                take()
            l_sc[s, :, cols] = alpha * l_sc[s, :, cols] + l_part
            a_sc[s, :, cols] = alpha * a_sc[s, :, cols] + a_part
            m_sc[s, :, cols] = m_new
            return nxt

        sts = [t() for t in score_thunks(*units[0])]
        for u, (s, c) in enumerate(units):
            sts = accumulate(sts, s, c, score_thunks(*units[u + 1]) if u + 1 < len(units) else [])

    @pl.when(kj < qi)
    def _():
        block(False)

    @pl.when(kj == qi)
    def _():
        block(True)
        lam = _diff_lambda(lq1_ref[...], lk1_ref[...], lq2_ref[...], lk2_ref[...], lam_init)
        ot = a_sc[0] / l_sc[0] - lam * (a_sc[1] / l_sc[1])
        ms = jnp.mean(ot * ot, axis=0, keepdims=True)
        y = ot * lax.rsqrt(ms + NORM_EPS) * sg_ref[...] * (1.0 - lam_init)
        o_ref[0] = y.T.astype(o_ref.dtype)


def _prompt_attn(qt, k, vt, lq1, lk1, lq2, lk2, subln_g_col, *, lam_init, n_heads, tq):
    B, S, _ = k.shape
    nq = S // tq
    pairs = [(i, j) for i in range(nq) for j in range(i + 1)]
    qi = jnp.asarray([p[0] for p in pairs], jnp.int32)
    kj = jnp.asarray([p[1] for p in pairs], jnp.int32)
    const = lambda b, h, p, qi, kj: (0, 0)
    grid_spec = pltpu.PrefetchScalarGridSpec(
        num_scalar_prefetch=2,
        grid=(B, n_heads, len(pairs)),
        in_specs=[
            pl.BlockSpec((1, V_DIM, tq), lambda b, h, p, qi, kj: (b, h, qi[p])),
            pl.BlockSpec((1, tq, V_DIM), lambda b, h, p, qi, kj: (b, kj[p], h)),
            pl.BlockSpec((1, V_DIM, tq), lambda b, h, p, qi, kj: (b, h, kj[p])),
            pl.BlockSpec((1, HEAD_DIM), const),
            pl.BlockSpec((1, HEAD_DIM), const),
            pl.BlockSpec((1, HEAD_DIM), const),
            pl.BlockSpec((1, HEAD_DIM), const),
            pl.BlockSpec((V_DIM, 1), const),
        ],
        out_specs=pl.BlockSpec((1, tq, V_DIM), lambda b, h, p, qi, kj: (b, qi[p], h)),
        scratch_shapes=[pltpu.VMEM((2, 1, tq), F32), pltpu.VMEM((2, 1, tq), F32), pltpu.VMEM((2, V_DIM, tq), F32)],
    )
    return pl.pallas_call(
        functools.partial(_prompt_attn_body, lam_init=lam_init, q_slab=ATTN_Q_SLAB, key_piece=ATTN_KEY_PIECE),
        grid_spec=grid_spec,
        out_shape=jax.ShapeDtypeStruct((B, S, n_heads * V_DIM), BF16),
        compiler_params=pltpu.CompilerParams(
            dimension_semantics=("parallel", "parallel", "arbitrary"), vmem_limit_bytes=VMEM_LIMIT),
        name="prompt_attn",
    )(qi, kj, qt, k, vt, lq1, lk1, lq2, lk2, subln_g_col)


def _paged_init(m_sc, l_sc, acc_sc):
    m_sc[...] = jnp.full(m_sc.shape, NEG_INF, F32)
    l_sc[...] = jnp.zeros(l_sc.shape, F32)
    acc_sc[...] = jnp.zeros(acc_sc.shape, F32)


def _paged_scores(qbd, k_refs):
    pairs = [jnp.concatenate([r[...].astype(BF16) for r in k_refs[i:i + 2]], axis=1) for i in range(0, len(k_refs), 2)]
    return jnp.concatenate([_dot(qbd, kk) for kk in pairs], axis=1)


def _paged_accumulate(s, t, v_refs, m_sc, l_sc, acc_sc, *, n_heads):
    page = v_refs[0].shape[0] // n_heads
    m_prev = m_sc[t]
    m_new = jnp.maximum(m_prev, jnp.max(s, axis=-1, keepdims=True))
    alpha = jnp.exp(m_prev - m_new)
    pr = jnp.exp(s - m_new)
    l_sc[t] = alpha * l_sc[t] + jnp.sum(pr, axis=-1, keepdims=True)
    prb = pr.astype(BF16)

    def head(h):
        return jnp.concatenate([r[pl.ds(h, page, stride=n_heads), :] for r in v_refs], axis=0).astype(BF16)

    for h in range(0, n_heads, 2):
        cols = slice(h * V_DIM, (h + 2) * V_DIM)
        acc_sc[t, :, cols] = alpha * acc_sc[t, :, cols] + _dot(prb, jnp.concatenate([head(h), head(h + 1)], axis=1))
    m_sc[t] = m_new


def _paged_finish(qbd, kn_ref, vn_ref, lam, sg_ref, o_ref, m_sc, l_sc, acc_sc, y_sc, *, lam_init, n_heads):
    n_streams, n_rows, _ = acc_sc.shape
    n_new = kn_ref.shape[1]
    half = n_rows // 2
    qf = qbd.astype(F32)
    q_tok = lax.broadcasted_iota(jnp.int32, (n_rows, 1), 0) % n_new
    s_new = []
    for j in range(n_new):
        kj = kn_ref[0, j:j + 1, :].astype(BF16).astype(F32)
        s_new.append(jnp.where(q_tok >= j, jnp.sum(qf * kj, axis=-1, keepdims=True), NEG_INF))
    m_fin = m_sc[0]
    for t in range(1, n_streams):
        m_fin = jnp.maximum(m_fin, m_sc[t])
    for sj in s_new:
        m_fin = jnp.maximum(m_fin, sj)
    l_fin = jnp.zeros_like(m_fin)
    acc = jnp.zeros(acc_sc.shape[1:], F32)
    for t in range(n_streams):
        alpha = jnp.exp(m_sc[t] - m_fin)
        l_fin = l_fin + alpha * l_sc[t]
        acc = acc + alpha * acc_sc[t]
    for j, sj in enumerate(s_new):
        pj = jnp.exp(sj - m_fin)
        l_fin = l_fin + pj
        acc = acc + pj.astype(BF16).astype(F32) * vn_ref[0, j:j + 1, :].astype(BF16).astype(F32)
    o = acc / l_fin
    o = o[:half] - lam * o[half:]
    for h in range(n_heads):
        blk = o[:, h * V_DIM:(h + 1) * V_DIM]
        y_sc[:, h * V_DIM:(h + 1) * V_DIM] = _rms(blk, sg_ref[...]) * (1.0 - lam_init)
    for h in range(n_heads):
        o_ref[0, :, h * V_DIM:(h + 1) * V_DIM] = y_sc[h * n_new:(h + 1) * n_new, h * V_DIM:(h + 1) * V_DIM]


def _mix_out_prompt_body(x_ref, o_ref, u_ref, vn_ref, ws_ref, bs_ref, wout_ref, gpost_ref, y_ref, gate_sc):
    tm = x_ref.shape[0]
    attn_w = o_ref.shape[1]
    gd = vn_ref.shape[1] // N_GROUPS
    row = lax.broadcasted_iota(jnp.int32, (CHUNK, CHUNK), 0)
    col = lax.broadcasted_iota(jnp.int32, (CHUNK, CHUNK), 1)
    for g in range(N_GROUPS):
        wg = jnp.where(col <= row, ws_ref[g], 0.0).astype(BF16)
        bias = bs_ref[:, g * gd:(g + 1) * gd]
        for c in range(tm // CHUNK):
            rows = slice(c * CHUNK, (c + 1) * CHUNK)
            cols = slice(g * gd, (g + 1) * gd)
            s = _dot(wg, vn_ref[rows, cols]) + bias
            gate_sc[rows, cols] = (u_ref[rows, cols] * s).astype(BF16)
    m = _dot(o_ref[...], wout_ref[:attn_w, :]) + _dot(gate_sc[...], wout_ref[attn_w:, :])
    y_ref[...] = x_ref[...] + _rms(m, gpost_ref[...])


def _mix_out_prompt(x2d, o2d, u2d, vn2d, w_spatial, bs_exp, w_out_bf, g_post, *, tm):
    n_rows, d_model = x2d.shape
    attn_w = o2d.shape[1]
    gm_w = u2d.shape[1]
    const2 = lambda i: (0, 0)
    rows = lambda i: (i, 0)
    return pl.pallas_call(
        _mix_out_prompt_body,
        grid=(n_rows // tm,),
        in_specs=[
            pl.BlockSpec((tm, d_model), rows),
            pl.BlockSpec((tm, attn_w), rows),
            pl.BlockSpec((tm, gm_w), rows),
            pl.BlockSpec((tm, gm_w), rows),
            pl.BlockSpec(w_spatial.shape, lambda i: (0, 0, 0)),
            pl.BlockSpec(bs_exp.shape, const2),
            pl.BlockSpec(w_out_bf.shape, const2),
            pl.BlockSpec((1, d_model), const2),
        ],
        out_specs=pl.BlockSpec((tm, d_model), rows),
        out_shape=jax.ShapeDtypeStruct((n_rows, d_model), F32),
        scratch_shapes=[pltpu.VMEM((tm, gm_w), BF16)],
        compiler_params=pltpu.CompilerParams(dimension_semantics=("parallel",),
                                             vmem_limit_bytes=VMEM_LIMIT),
        name="mix_out_prompt",
    )(x2d, o2d, u2d, vn2d, w_spatial, bs_exp, w_out_bf, g_post)


def _mix_out_sample_body(x_ref, o_ref, u_ref, vn_ref, coef_ref, bias_ref, wout_ref, gpost_ref, y_ref, hist):
    t = pl.program_id(0)
    n_t = hist.shape[0]
    attn_w = o_ref.shape[1]

    @pl.when(t == 0)
    def _():
        hist[...] = jnp.zeros(hist.shape, F32)

    hist[t] = vn_ref[...]
    s = bias_ref[pl.ds(t, 1), :]
    for j in range(n_t):
        cj = jnp.where(j <= t, coef_ref[pl.ds(t * n_t + j, 1), :], 0.0)
        s = s + cj.astype(BF16).astype(F32) * hist[j].astype(BF16).astype(F32)
    gate = (u_ref[...] * s).astype(BF16)
    m = _dot(o_ref[...].astype(BF16), wout_ref[:attn_w, :]) + _dot(gate, wout_ref[attn_w:, :])
    y_ref[...] = x_ref[...] + _rms(m, gpost_ref[...])


def _mix_out_sample(x2d, o2d, u2d, vn2d, coef, bias, w_out_bf, g_post, *, n_t):
    n_dec = x2d.shape[0]
    d_model = x2d.shape[1] // n_t
    attn_w = o2d.shape[1] // n_t
    gm_w = u2d.shape[1] // n_t
    const2 = lambda t: (0, 0)
    colblk = lambda t: (0, t)
    return pl.pallas_call(
        _mix_out_sample_body,
        grid=(n_t,),
        in_specs=[
            pl.BlockSpec((n_dec, d_model), colblk),
            pl.BlockSpec((n_dec, attn_w), colblk),
            pl.BlockSpec((n_dec, gm_w), colblk),
            pl.BlockSpec((n_dec, gm_w), colblk),
            pl.BlockSpec(coef.shape, const2),
            pl.BlockSpec(bias.shape, const2),
            pl.BlockSpec(w_out_bf.shape, const2),
            pl.BlockSpec((1, d_model), const2),
        ],
        out_specs=pl.BlockSpec((n_dec, d_model), colblk),
        out_shape=jax.ShapeDtypeStruct(x2d.shape, F32),
        scratch_shapes=[pltpu.VMEM((n_t, n_dec, gm_w), F32)],
        compiler_params=pltpu.CompilerParams(dimension_semantics=("arbitrary",),
                                             vmem_limit_bytes=VMEM_LIMIT),
        name="mix_out_sample",
    )(x2d, o2d, u2d, vn2d, coef, bias, w_out_bf, g_post)


def _ffn_chunks(h, taps, cw_ref, cb_ref, wup_ref, wdn_ref, *, d_ff, col_chunk, chunk_ids, between=()):
    gate_cols = lambda c: slice(c * col_chunk, (c + 1) * col_chunk)
    val_cols = lambda c: slice(d_ff + c * col_chunk, d_ff + (c + 1) * col_chunk)

    def up(c):
        return _dot(h, wup_ref[:, gate_cols(c)]), _dot(h, wup_ref[:, val_cols(c)])

    def conv(u, cols):
        shifted = taps(u, cols)
        y = cb_ref[:, cols] + cw_ref[0:1, cols] * shifted[0]
        for j in range(1, CONV_W):
            y = y + cw_ref[j:j + 1, cols] * shifted[j]
        return y

    def down(a, c):
        return _dot(a, wdn_ref[gate_cols(c), :])

    acc = None
    act = None
    up_next = up(chunk_ids[0])
    for i, c in enumerate(chunk_ids):
        up_gate, up_val = up_next
        if i + 1 < len(chunk_ids):
            up_next = up(chunk_ids[i + 1])
        if act is not None:
            part = down(act, chunk_ids[i - 1])
            acc = part if acc is None else acc + part
        act = (_gelu(conv(up_gate, gate_cols(c))) * conv(up_val, val_cols(c))).astype(BF16)
        if i < len(between):
            between[i]()
    part = down(act, chunk_ids[-1])
    return part if acc is None else acc + part


def _ffn_prompt_body(pt_ref, x_ref, g_ref, wup_ref, cw_ref, cb_ref, wdn_ref, gpost_ref,
                     qbd_ref, kn_ref, vn_ref, lq1_ref, lk1_ref, lq2_ref, lk2_ref, sg_ref, kt_hbm, v_hbm,
                     y_ref, conv_ref, o_ref,
                     carry1, carry2, h_sc, acc_ffn, m_sc, l_sc, acc_sc, y_sc, kbuf, vbuf, sem, *,
                     d_ff, col_chunk, lam_init, n_pages_step, n_streams, n_heads, chunk_groups):
    b = pl.program_id(0)
    j = pl.program_id(1)
    sub = pl.program_id(2)
    n_sub = len(chunk_groups)
    step = (b * pl.num_programs(1) + j) * n_sub + sub
    last_step = pl.num_programs(0) * pl.num_programs(1) * n_sub - 1
    slot = step % 2
    per = n_pages_step // n_streams
    row = lax.broadcasted_iota(jnp.int32, (SUBLANES, 1), 0)

    def page_copies(of_step, into_slot):
        seq = of_step // n_sub
        first_page = (of_step % n_sub) * n_pages_step
        copies = []
        for p in range(n_pages_step):
            phys = pt_ref[seq, first_page + p]
            copies.append(pltpu.make_async_copy(kt_hbm.at[phys], kbuf.at[into_slot, p], sem.at[into_slot, 0]))
            copies.append(pltpu.make_async_copy(v_hbm.at[phys], vbuf.at[into_slot, p], sem.at[into_slot, 1]))
        return copies

    def wait_slot(which):
        pltpu.make_async_copy(kbuf.at[which], kbuf.at[which], sem.at[which, 0]).wait()
        pltpu.make_async_copy(vbuf.at[which], vbuf.at[which], sem.at[which, 1]).wait()

    @pl.when(step == 0)
    def _():
        for c in page_copies(step, slot):
            c.start()

    k_refs = [kbuf.at[slot, p] for p in range(n_pages_step)]
    v_refs = [vbuf.at[slot, p] for p in range(n_pages_step)]

    @pl.when((j == 0) & (sub == 0))
    def _():
        carry1[...] = jnp.zeros(carry1.shape, F32)
        carry2[...] = jnp.zeros(carry2.shape, F32)

    def taps(up, cols):
        r1 = pltpu.roll(up, 1, 0)
        r2 = pltpu.roll(up, 2, 0)
        t1 = jnp.concatenate([jnp.where(row < 1, carry1[:, cols], r1[:SUBLANES]), r1[SUBLANES:]], axis=0)
        t2 = jnp.concatenate([jnp.where(row < 2, carry2[:, cols], r2[:SUBLANES]), r2[SUBLANES:]], axis=0)
        carry1[:, cols] = r1[:SUBLANES]
        carry2[:, cols] = r2[:SUBLANES]
        conv_ref[0, :, cols] = r2[:CONV_W - 1]
        return [t2, t1, up]

    def sub_step(grp, chunk_ids):
        first, last = grp == 0, grp == len(chunk_groups) - 1
        for c in page_copies(jnp.minimum(step + 1, last_step), 1 - slot):
            c.start()
        qbd = qbd_ref[0]
        if first:
            h = _rms(x_ref[0], g_ref[...]).astype(BF16)
            h_sc[...] = h
            _paged_init(m_sc, l_sc, acc_sc)
        else:
            h = h_sc[...]
        wait_slot(slot)
        scores = [None] * n_streams
        scores[0] = _paged_scores(qbd, k_refs[:per])

        def stream(t):
            def run():
                _paged_accumulate(scores[t], t, v_refs[t * per:(t + 1) * per], m_sc, l_sc, acc_sc, n_heads=n_heads)
                if t + 1 < n_streams:
                    scores[t + 1] = _paged_scores(qbd, k_refs[(t + 1) * per:(t + 2) * per])
            return run

        n_between = min(n_streams, len(chunk_ids))
        part = _ffn_chunks(h, taps, cw_ref, cb_ref, wup_ref, wdn_ref, d_ff=d_ff, col_chunk=col_chunk,
                           chunk_ids=chunk_ids, between=[stream(t) for t in range(n_between)])
        for t in range(n_between, n_streams):
            stream(t)()
        if first:
            acc_ffn[...] = part
        elif not last:
            acc_ffn[...] = acc_ffn[...] + part
        if last:
            total = part if first else acc_ffn[...] + part
            y_ref[0] = x_ref[0] + _rms(total, gpost_ref[...])
            lam = _diff_lambda(lq1_ref[...], lk1_ref[...], lq2_ref[...], lk2_ref[...], lam_init)
            _paged_finish(qbd, kn_ref, vn_ref, lam, sg_ref, o_ref, m_sc, l_sc, acc_sc, y_sc,
                          lam_init=lam_init, n_heads=n_heads)

    for grp, chunk_ids in enumerate(chunk_groups):
        pl.when(sub == grp)(functools.partial(sub_step, grp, chunk_ids))

    @pl.when(step == last_step)
    def _():
        wait_slot(1 - slot)


def _ffn_prompt_paged(x3d, g_pre, w_up_bf, conv_w, conv_b, w_down_bf, g_post,
                      page_table, qbd, k_new, v_new, cache_kt, cache_v, lq1, lk1, lq2, lk2, subln_g, *,
                      tm, col_chunk, lam_init, n_heads, n_pages_step, n_streams):
    B, S, d_model = x3d.shape
    d_ff = w_down_bf.shape[0]
    n_dec, n_pages = page_table.shape
    _, width, page = cache_kt.shape
    n_new = k_new.shape[1]
    n_rows = qbd.shape[1]
    G = n_pages_step
    tiles = S // tm
    n_sub = n_pages // G
    n_chunks = d_ff // col_chunk
    assert B * tiles == n_dec and n_pages % G == 0 and G % (2 * n_streams) == 0 and n_sub <= n_chunks
    assert n_heads % 2 == 0
    bounds = [round(i * n_chunks / n_sub) for i in range(n_sub + 1)]
    chunk_groups = tuple(tuple(range(bounds[i], bounds[i + 1])) for i in range(n_sub))

    const2 = lambda b, j, s, pt: (0, 0)
    tile = lambda b, j, s, pt: (b, j, 0)
    per_seq = lambda b, j, s, pt: (b * tiles + j, 0, 0)

    grid_spec = pltpu.PrefetchScalarGridSpec(
        num_scalar_prefetch=1,
        grid=(B, tiles, n_sub),
        in_specs=[
            pl.BlockSpec((1, tm, d_model), tile),
            pl.BlockSpec((1, d_model), const2),
            pl.BlockSpec(w_up_bf.shape, const2, pipeline_mode=pl.Buffered(1)),
            pl.BlockSpec(conv_w.shape, const2),
            pl.BlockSpec(conv_b.shape, const2),
            pl.BlockSpec(w_down_bf.shape, const2, pipeline_mode=pl.Buffered(1)),
            pl.BlockSpec((1, d_model), const2),
            pl.BlockSpec((1, n_rows, width), per_seq),
            pl.BlockSpec((1, n_new, width), per_seq),
            pl.BlockSpec((1, n_new, width), per_seq),
            pl.BlockSpec((1, HEAD_DIM), const2),
            pl.BlockSpec((1, HEAD_DIM), const2),
            pl.BlockSpec((1, HEAD_DIM), const2),
            pl.BlockSpec((1, HEAD_DIM), const2),
            pl.BlockSpec((1, V_DIM), const2),
            pl.BlockSpec(memory_space=pl.ANY),
            pl.BlockSpec(memory_space=pl.ANY),
        ],
        out_specs=[
            pl.BlockSpec((1, tm, d_model), tile),
            pl.BlockSpec((1, CONV_W - 1, 2 * d_ff), lambda b, j, s, pt: (b, 0, 0)),
            pl.BlockSpec((1, n_new, width), per_seq),
        ],
        scratch_shapes=[
            pltpu.VMEM((SUBLANES, 2 * d_ff), F32), pltpu.VMEM((SUBLANES, 2 * d_ff), F32),
            pltpu.VMEM((tm, d_model), BF16), pltpu.VMEM((tm, d_model), F32),
            pltpu.VMEM((n_streams, n_rows, 1), F32), pltpu.VMEM((n_streams, n_rows, 1), F32),
            pltpu.VMEM((n_streams, n_rows, width), F32), pltpu.VMEM((n_rows // 2, width), F32),
            pltpu.VMEM((2, G, width, page), F32), pltpu.VMEM((2, G, page * n_heads, V_DIM), F32),
            pltpu.SemaphoreType.DMA((2, 2)),
        ],
    )
    return pl.pallas_call(
        functools.partial(_ffn_prompt_body, d_ff=d_ff, col_chunk=col_chunk, lam_init=lam_init, n_pages_step=G,
                          n_streams=n_streams, n_heads=n_heads, chunk_groups=chunk_groups),
        grid_spec=grid_spec,
        out_shape=[
            jax.ShapeDtypeStruct((B, S, d_model), F32),
            jax.ShapeDtypeStruct((B, CONV_W - 1, 2 * d_ff), F32),
            jax.ShapeDtypeStruct((n_dec, n_new, width), F32),
        ],
        compiler_params=pltpu.CompilerParams(dimension_semantics=("arbitrary", "arbitrary", "arbitrary"),
                                             vmem_limit_bytes=VMEM_LIMIT),
        name="ffn_prompt_paged",
    )(page_table, x3d, g_pre, w_up_bf, conv_w, conv_b, w_down_bf, g_post,
      qbd, k_new, v_new, lq1, lk1, lq2, lk2, subln_g, cache_kt, cache_v)


def _ffn_sample_body(x_ref, st_ref, g_ref, wup_ref, cw_ref, cb_ref, wdn_ref, gpost_ref, y_ref, conv_ref, ring, *,
                     d_ff, col_chunk):
    t = pl.program_id(0)
    width = 2 * d_ff

    @pl.when(t == 0)
    def _():
        for j in range(CONV_W - 1):
            ring[j] = st_ref[:, j * width:(j + 1) * width]

    x = x_ref[...]
    h = _rms(x, g_ref[...]).astype(BF16)
    cur = (t + CONV_W - 1) % CONV_W

    def taps(up, cols):
        ring[cur, :, cols] = up
        conv_ref[:, cols] = up
        return [ring[(t + j) % CONV_W, :, cols] for j in range(CONV_W - 1)] + [up]

    acc = _ffn_chunks(h, taps, cw_ref, cb_ref, wup_ref, wdn_ref, d_ff=d_ff, col_chunk=col_chunk,
                      chunk_ids=tuple(range(d_ff // col_chunk)))
    y_ref[...] = x + _rms(acc, gpost_ref[...])


def _ffn_sample(x2d, state2d, g_pre, w_up_bf, conv_w, conv_b, w_down_bf, g_post, *, n_t, col_chunk):
    n_dec = x2d.shape[0]
    d_model = x2d.shape[1] // n_t
    d_ff = w_down_bf.shape[0]
    width = 2 * d_ff
    const2 = lambda t: (0, 0)
    colblk = lambda t: (0, t)
    conv_map = lambda t: (0, jnp.maximum(t - (n_t - (CONV_W - 1)), 0))
    return pl.pallas_call(
        functools.partial(_ffn_sample_body, d_ff=d_ff, col_chunk=col_chunk),
        grid=(n_t,),
        in_specs=[
            pl.BlockSpec((n_dec, d_model), colblk),
            pl.BlockSpec(state2d.shape, const2),
            pl.BlockSpec((1, d_model), const2),
            pl.BlockSpec(w_up_bf.shape, const2, pipeline_mode=pl.Buffered(1)),
            pl.BlockSpec(conv_w.shape, const2),
            pl.BlockSpec(conv_b.shape, const2),
            pl.BlockSpec(w_down_bf.shape, const2, pipeline_mode=pl.Buffered(1)),
            pl.BlockSpec((1, d_model), const2),
        ],
        out_specs=[
            pl.BlockSpec((n_dec, d_model), colblk),
            pl.BlockSpec((n_dec, width), conv_map),
        ],
        out_shape=[
            jax.ShapeDtypeStruct(x2d.shape, F32),
            jax.ShapeDtypeStruct((n_dec, (CONV_W - 1) * width), F32),
        ],
        scratch_shapes=[pltpu.VMEM((CONV_W, n_dec, width), F32)],
        compiler_params=pltpu.CompilerParams(dimension_semantics=("arbitrary",),
                                             vmem_limit_bytes=VMEM_LIMIT),
        name="ffn_sample",
    )(x2d, state2d, g_pre, w_up_bf, conv_w, conv_b, w_down_bf, g_post)


def _rope_tables(pos):
    half = HEAD_DIM // 2
    inv = ROPE_THETA ** (-jnp.arange(half, dtype=F32) * 2.0 / HEAD_DIM)
    ang = pos[:, None] * inv[None, :]
    cos = jnp.cos(ang)
    sin = jnp.sin(ang)
    cos_t = jnp.concatenate([cos, cos, cos, cos], axis=1)
    sin_t = jnp.concatenate([-sin, sin, -sin, sin], axis=1)
    return cos_t, sin_t


def kernel(x_prompt, x_sample, cache_k, cache_v, state_conv, page_table, norm_mix_pre, w_in, lambda_q1,
           lambda_k1, lambda_q2, lambda_k2, subln_g, gate_ln_g, gate_ln_b, w_spatial, b_spatial, w_out,
           norm_mix_post, norm_ffn_pre, w_up, conv_w, conv_b, w_down, norm_ffn_post):
    B, S, d_model = x_prompt.shape
    n_dec, n_t, _ = x_sample.shape
    depth, n_phys, page, n_heads, _, _ = cache_k.shape
    qk_w = n_heads * 2 * HEAD_DIM
    attn_w = n_heads * V_DIM
    gm_w = (w_in.shape[2] - 2 * qk_w - attn_w) // 2
    d_ff = w_down.shape[1]
    past_len = page_table.shape[1] * page
    assert depth == 1 and w_spatial.shape[1] == N_GROUPS and w_spatial.shape[2] == CHUNK
    assert CONV_W - 1 <= n_t <= CHUNK and conv_w.shape[1] == CONV_W
    widths = dict(qk_w=qk_w, attn_w=attn_w, gm_w=gm_w)

    l = 0
    lam_init = 0.8 - 0.6 * math.exp(-0.3 * l)
    row = lambda a: a[l].reshape(1, -1)
    w_in_bf = w_in[l].astype(BF16)
    w_out_bf = w_out[l].astype(BF16)
    w_up_bf = w_up[l].astype(BF16)
    w_down_bf = w_down[l].astype(BF16)
    lq1, lk1, lq2, lk2 = row(lambda_q1), row(lambda_k1), row(lambda_q2), row(lambda_k2)
    sg = row(subln_g)
    g_mix_pre, g_mix_post = row(norm_mix_pre), row(norm_mix_post)
    g_ffn_pre, g_ffn_post = row(norm_ffn_pre), row(norm_ffn_post)
    ln_g, ln_b = row(gate_ln_g), row(gate_ln_b)
    conv_b2 = row(conv_b)
    gd = gm_w // N_GROUPS

    cos_p, sin_p = _rope_tables(jnp.arange(S, dtype=F32))
    qt_p, kt_p, kb_p, vf_p, vt_p, u_p, vn_p = _mix_in_prompt(
        x_prompt, g_mix_pre, w_in_bf, cos_p, sin_p, ln_g, ln_b, tm=TILE_MIX_IN, **widths)
    o_p = _prompt_attn(qt_p, kb_p, vt_p, lq1, lk1, lq2, lk2, sg.reshape(V_DIM, 1),
                       lam_init=lam_init, n_heads=n_heads, tq=TILE_ATTN)
    bs_exp = jnp.repeat(b_spatial[l].T, gd, axis=1)
    x1_p = _mix_out_prompt(x_prompt.reshape(B * S, d_model), o_p.reshape(B * S, attn_w),
                           u_p.reshape(B * S, gm_w), vn_p.reshape(B * S, gm_w), w_spatial[l], bs_exp, w_out_bf,
                           g_mix_post, tm=TILE_MIX_OUT)
    new_k_p = kt_p.reshape(B, n_heads, 2, HEAD_DIM, S).transpose(0, 4, 1, 2, 3)[None]

    cos_s, sin_s = _rope_tables(jnp.arange(n_t, dtype=F32) + float(past_len))
    cos_s = jnp.repeat(cos_s, n_dec, axis=0)
    sin_s = jnp.repeat(sin_s, n_dec, axis=0)
    xs2d = x_sample.reshape(n_dec, n_t * d_model)
    q_s, kf_s, vf_s, u_s, vn_s = _mix_in_sample(xs2d, g_mix_pre, w_in_bf, cos_s, sin_s, ln_g, ln_b, n_t=n_t,
                                                **widths)
    q_hs = q_s.reshape(n_dec, n_t, n_heads, 2, HEAD_DIM).transpose(0, 3, 2, 1, 4)
    same = ((jnp.arange(2)[:, None, None, None] == jnp.arange(2)[None, None, None, :])
            & (jnp.arange(n_heads)[None, :, None, None] == jnp.arange(n_heads)[None, None, :, None]))
    qbd = jnp.where(same[None, :, :, None, :, :, None], q_hs[:, :, :, :, None, None, :], jnp.zeros((), BF16))
    qbd = qbd.reshape(n_dec, 2 * n_heads * n_t, qk_w)
    cache_kt = cache_k[l].transpose(0, 2, 3, 4, 1).reshape(n_phys, qk_w, page)
    cache_v2 = cache_v[l].reshape(n_phys, page * n_heads, V_DIM)

    y_p, conv_p, o_s = _ffn_prompt_paged(
        x1_p.reshape(B, S, d_model), g_ffn_pre, w_up_bf, conv_w[l], conv_b2, w_down_bf, g_ffn_post,
        page_table, qbd, kf_s.reshape(n_dec, n_t, qk_w), vf_s.reshape(n_dec, n_t, attn_w), cache_kt, cache_v2,
        lq1, lk1, lq2, lk2, sg, tm=TILE_FFN, col_chunk=FFN_COL_CHUNK, lam_init=lam_init, n_heads=n_heads,
        n_pages_step=PAGES_PER_STEP, n_streams=PAGE_STREAMS)

    coef = jnp.repeat(w_spatial[l][:, :n_t, :n_t].transpose(1, 2, 0).reshape(n_t * n_t, N_GROUPS), gd, axis=1)
    bias = jnp.repeat(b_spatial[l][:, :n_t].T, gd, axis=1)
    x1_s = _mix_out_sample(xs2d, o_s.reshape(n_dec, n_t * attn_w), u_s, vn_s, coef, bias, w_out_bf, g_mix_post,
                           n_t=n_t)
    y_s, conv_s = _ffn_sample(x1_s, state_conv[l].reshape(n_dec, (CONV_W - 1) * 2 * d_ff), g_ffn_pre, w_up_bf,
                              conv_w[l], conv_b2, w_down_bf, g_ffn_post, n_t=n_t, col_chunk=FFN_COL_CHUNK)

    return (y_p,
            y_s.reshape(n_dec, n_t, d_model),
            new_k_p,
            vf_p.reshape(1, B, S, n_heads, V_DIM),
            conv_p.reshape(1, B, CONV_W - 1, 2 * d_ff),
            kf_s.reshape(1, n_dec, n_t, n_heads, 2, HEAD_DIM),
            vf_s.reshape(1, n_dec, n_t, n_heads, V_DIM),
            conv_s.reshape(1, n_dec, CONV_W - 1, 2 * d_ff),
            vn_s.reshape(1, n_dec, n_t, gm_w))
```

```python
import functools
import math

import jax
import jax.numpy as jnp
from jax import lax
from jax.experimental import pallas as pl
from jax.experimental.pallas import tpu as pltpu

F32 = jnp.float32
BF16 = jnp.bfloat16

HEAD_DIM = 64
V_DIM = 2 * HEAD_DIM
CHUNK = 128
N_GROUPS = 4
CONV_W = 3
ROPE_THETA = 10000.0
NORM_EPS = 1e-6
NEG_INF = -1e30
LANES = 128
SUBLANES = 8
VMEM_LIMIT = 56 * 1024 * 1024
VMEM_LIMIT_FUSED = 62 * 1024 * 1024

TILE_MIX_IN = 1024
TILE_ATTN = 2048
ATTN_Q_SLAB = 256
ATTN_KEY_PIECE = 512
TILE_MIX_OUT = 1024
TILE_FFN = 256
FFN_COL_CHUNK = 256
PAGES_PER_STEP = 32
PAGE_STREAMS = 2


def _rms(x, g):
    return x * lax.rsqrt(jnp.mean(x * x, axis=-1, keepdims=True) + NORM_EPS) * g


def _gelu(x):
    c = math.sqrt(2.0 / math.pi)
    hx = 0.5 * x
    return hx + hx * jnp.tanh(x * (c + (c * 0.044715) * (x * x)))


def _dot(a, b):
    return jnp.dot(a, b, preferred_element_type=F32)


def _diff_lambda(lq1, lk1, lq2, lk2, lam_init):
    a = jnp.sum(lq1 * lk1, axis=-1, keepdims=True)
    b = jnp.sum(lq2 * lk2, axis=-1, keepdims=True)
    return jnp.exp(a) - jnp.exp(b) + lam_init


def _rope(z, cos, sin_signed, lo_half):
    outs = []
    for c in range(z.shape[1] // LANES):
        xs = z[:, c * LANES:(c + 1) * LANES]
        ahead = pltpu.roll(xs, LANES - HEAD_DIM // 2, 1)
        behind = pltpu.roll(xs, HEAD_DIM // 2, 1)
        outs.append(xs * cos + jnp.where(lo_half, ahead, behind) * sin_signed)
    return jnp.concatenate(outs, axis=1)


def _mix_in_core(x, g_ref, w_ref, cos_ref, sin_ref, lng_ref, lnb_ref, *, qk_w, attn_w, gm_w, q_scale):
    h = _rms(x, g_ref[...]).astype(BF16)
    cos = cos_ref[...]
    sin_signed = sin_ref[...]
    lane = lax.broadcasted_iota(jnp.int32, (1, LANES), 1)
    lo_half = (lane % HEAD_DIM) < (HEAD_DIM // 2)
    c0 = 0
    q = _rope(_dot(h, w_ref[:, c0:c0 + qk_w]), cos, sin_signed, lo_half) * q_scale
    c0 += qk_w
    k = _rope(_dot(h, w_ref[:, c0:c0 + qk_w]), cos, sin_signed, lo_half)
    c0 += qk_w
    v = _dot(h, w_ref[:, c0:c0 + attn_w])
    c0 += attn_w
    u = _gelu(_dot(h, w_ref[:, c0:c0 + gm_w]))
    c0 += gm_w
    gv = _gelu(_dot(h, w_ref[:, c0:c0 + gm_w]))
    mu = jnp.mean(gv, axis=-1, keepdims=True)
    gc = gv - mu
    vn = gc * lax.rsqrt(jnp.mean(gc * gc, axis=-1, keepdims=True) + NORM_EPS)
    return q, k, v, u, vn * lng_ref[...] + lnb_ref[...]


def _mix_in_prompt_body(x_ref, g_ref, w_ref, cos_ref, sin_ref, lng_ref, lnb_ref,
                        qt_ref, kt_ref, kb_ref, vf_ref, vt_ref, u_ref, vn_ref, **widths):
    q, k, v, u, vn = _mix_in_core(x_ref[0], g_ref, w_ref, cos_ref, sin_ref, lng_ref, lnb_ref, **widths)
    qt_ref[0] = q.T.astype(BF16)
    kt_ref[0] = k.T
    kb_ref[0] = k.astype(BF16)
    tm = v.shape[0]
    n_heads = v.shape[1] // V_DIM
    for h in range(n_heads):
        vf_ref[pl.ds(0, 1), pl.ds(h, tm, stride=n_heads), :] = v[:, h * V_DIM:(h + 1) * V_DIM][None]
    vt_ref[0] = v.T.astype(BF16)
    u_ref[0] = u
    vn_ref[0] = vn.astype(BF16)


def _mix_in_prompt(x3d, g_pre, w_in_bf, cos_t, sin_t, ln_g, ln_b, *, tm, qk_w, attn_w, gm_w):
    B, S, d_model = x3d.shape
    const = lambda b, j: (0, 0)
    rows = lambda b, j: (b, j, 0)
    cols = lambda b, j: (b, 0, j)

    def natural(width, dtype):
        return jax.ShapeDtypeStruct((B, S, width), dtype), pl.BlockSpec((1, tm, width), rows)

    def transposed(width, dtype):
        return jax.ShapeDtypeStruct((B, width, S), dtype), pl.BlockSpec((1, width, tm), cols)

    n_heads = attn_w // V_DIM
    v_rows = (jax.ShapeDtypeStruct((B, S * n_heads, V_DIM), F32), pl.BlockSpec((1, tm * n_heads, V_DIM), rows))
    outs = [transposed(qk_w, BF16), transposed(qk_w, F32), natural(qk_w, BF16), v_rows,
            transposed(attn_w, BF16), natural(gm_w, F32), natural(gm_w, BF16)]
    return pl.pallas_call(
        functools.partial(_mix_in_prompt_body, qk_w=qk_w, attn_w=attn_w, gm_w=gm_w,
                          q_scale=HEAD_DIM ** -0.5 * math.log2(math.e)),
        grid=(B, S // tm),
        in_specs=[
            pl.BlockSpec((1, tm, d_model), rows),
            pl.BlockSpec((1, d_model), const),
            pl.BlockSpec(w_in_bf.shape, const),
            pl.BlockSpec((tm, LANES), lambda b, j: (j, 0)),
            pl.BlockSpec((tm, LANES), lambda b, j: (j, 0)),
            pl.BlockSpec((1, gm_w), const),
            pl.BlockSpec((1, gm_w), const),
        ],
        out_specs=[o[1] for o in outs],
        out_shape=[o[0] for o in outs],
        compiler_params=pltpu.CompilerParams(dimension_semantics=("parallel", "parallel"),
                                             vmem_limit_bytes=VMEM_LIMIT),
        name="mix_in_prompt",
    )(x3d, g_pre, w_in_bf, cos_t, sin_t, ln_g, ln_b)


def _mix_in_sample_body(x_ref, g_ref, w_ref, cos_ref, sin_ref, lng_ref, lnb_ref,
                        q_ref, k_ref, v_ref, u_ref, vn_ref, **widths):
    q, k, v, u, vn = _mix_in_core(x_ref[...], g_ref, w_ref, cos_ref, sin_ref, lng_ref, lnb_ref, **widths)
    q_ref[...] = q.astype(BF16)
    k_ref[...] = k
    v_ref[...] = v
    u_ref[...] = u
    vn_ref[...] = vn


def _mix_in_sample(x2d, g_pre, w_in_bf, cos_t, sin_t, ln_g, ln_b, *, n_t, qk_w, attn_w, gm_w):
    n_dec = x2d.shape[0]
    d_model = w_in_bf.shape[0]
    const = lambda t: (0, 0)
    colblk = lambda t: (0, t)

    def out(width, dtype):
        return jax.ShapeDtypeStruct((n_dec, n_t * width), dtype), pl.BlockSpec((n_dec, width), colblk)

    outs = [out(qk_w, BF16), out(qk_w, F32), out(attn_w, F32), out(gm_w, F32), out(gm_w, F32)]
    return pl.pallas_call(
        functools.partial(_mix_in_sample_body, qk_w=qk_w, attn_w=attn_w, gm_w=gm_w, q_scale=HEAD_DIM ** -0.5),
        grid=(n_t,),
        in_specs=[
            pl.BlockSpec((n_dec, d_model), colblk),
            pl.BlockSpec((1, d_model), const),
            pl.BlockSpec(w_in_bf.shape, const),
            pl.BlockSpec((n_dec, LANES), lambda t: (t, 0)),
            pl.BlockSpec((n_dec, LANES), lambda t: (t, 0)),
            pl.BlockSpec((1, gm_w), const),
            pl.BlockSpec((1, gm_w), const),
        ],
        out_specs=[o[1] for o in outs],
        out_shape=[o[0] for o in outs],
        compiler_params=pltpu.CompilerParams(dimension_semantics=("parallel",),
                                             vmem_limit_bytes=VMEM_LIMIT),
        name="mix_in_sample",
    )(x2d, g_pre, w_in_bf, cos_t, sin_t, ln_g, ln_b)


def _prompt_attn_body(qi_ref, kj_ref, qt_ref, k_ref, vt_ref, lq1_ref, lk1_ref, lq2_ref, lk2_ref, sg_ref,
                      o_ref, m_sc, l_sc, a_sc, *, lam_init, q_slab, key_piece):
    p = pl.program_id(2)
    qi = qi_ref[p]
    kj = kj_ref[p]
    tq = qt_ref.shape[2]
    tk = k_ref.shape[1]
    units = [(s, c) for s in range(2) for c in range(tq // q_slab)]

    @pl.when(kj == 0)
    def _():
        m_sc[...] = jnp.full(m_sc.shape, NEG_INF, F32)
        l_sc[...] = jnp.zeros(l_sc.shape, F32)
        a_sc[...] = jnp.zeros(a_sc.shape, F32)

    def block(diagonal):
        n_keys = (lambda c: (c + 1) * q_slab) if diagonal else (lambda c: tk)

        def key_pieces(c):
            return [(k0, min(k0 + key_piece, n_keys(c))) for k0 in range(0, n_keys(c), key_piece)]

        def score_thunks(s, c):
            qs = qt_ref[0, :, c * q_slab:(c + 1) * q_slab]
            zeros = jnp.zeros((HEAD_DIM, q_slab), BF16)
            q_half = (jnp.concatenate([qs[:HEAD_DIM], zeros], axis=0) if s == 0
                      else jnp.concatenate([zeros, qs[HEAD_DIM:]], axis=0))

            def piece(k0, k1):
                st = _dot(k_ref[0, k0:k1, :], q_half)
                if diagonal and k1 > c * q_slab:
                    key = lax.broadcasted_iota(jnp.int32, st.shape, 0) + k0
                    qry = lax.broadcasted_iota(jnp.int32, st.shape, 1) + c * q_slab
                    st = jnp.where(key <= qry, st, NEG_INF)
                return st
            return [functools.partial(piece, k0, k1) for k0, k1 in key_pieces(c)]

        def accumulate(sts, s, c, next_thunks):
            cols = slice(c * q_slab, (c + 1) * q_slab)
            nxt = []
            take = lambda: nxt.append(next_thunks.pop(0)()) if next_thunks else None
            take()
            m_prev = m_sc[s, :, cols]
            m_new = m_prev
            for st in sts:
                m_new = jnp.maximum(m_new, jnp.max(st, axis=0, keepdims=True))
            alpha = jnp.exp2(m_prev - m_new)
            l_part = None
            a_part = None
            for st, (k0, k1) in zip(sts, key_pieces(c)):
                take()
                pt = jnp.exp2(st - m_new)
                lsum = jnp.sum(pt, axis=0, keepdims=True)
                pv = _dot(vt_ref[0, :, k0:k1], pt.astype(BF16))
                l_part = lsum if l_part is None else l_part + lsum
                a_part = pv if a_part is None else a_part + pv
            while next_thunks:
                take()
            l_sc[s, :, cols] = alpha * l_sc[s, :, cols] + l_part
            a_sc[s, :, cols] = alpha * a_sc[s, :, cols] + a_part
            m_sc[s, :, cols] = m_new
            return nxt

        sts = [t() for t in score_thunks(*units[0])]
        for u, (s, c) in enumerate(units):
            sts = accumulate(sts, s, c, score_thunks(*units[u + 1]) if u + 1 < len(units) else [])

    @pl.when(kj < qi)
    def _():
        block(False)

    @pl.when(kj == qi)
    def _():
        block(True)
        lam = _diff_lambda(lq1_ref[...], lk1_ref[...], lq2_ref[...], lk2_ref[...], lam_init)
        ot = a_sc[0] / l_sc[0] - lam * (a_sc[1] / l_sc[1])
        ms = jnp.mean(ot * ot, axis=0, keepdims=True)
        y = ot * lax.rsqrt(ms + NORM_EPS) * sg_ref[...] * (1.0 - lam_init)
        o_ref[0] = y.T.astype(o_ref.dtype)


def _prompt_attn(qt, k, vt, lq1, lk1, lq2, lk2, subln_g_col, *, lam_init, n_heads, tq):
    B, S, _ = k.shape
    nq = S // tq
    pairs = [(i, j) for i in range(nq) for j in range(i + 1)]
    qi = jnp.asarray([p[0] for p in pairs], jnp.int32)
    kj = jnp.asarray([p[1] for p in pairs], jnp.int32)
    const = lambda b, h, p, qi, kj: (0, 0)
    grid_spec = pltpu.PrefetchScalarGridSpec(
        num_scalar_prefetch=2,
        grid=(B, n_heads, len(pairs)),
        in_specs=[
            pl.BlockSpec((1, V_DIM, tq), lambda b, h, p, qi, kj: (b, h, qi[p])),
            pl.BlockSpec((1, tq, V_DIM), lambda b, h, p, qi, kj: (b, kj[p], h)),
            pl.BlockSpec((1, V_DIM, tq), lambda b, h, p, qi, kj: (b, h, kj[p])),
            pl.BlockSpec((1, HEAD_DIM), const),
            pl.BlockSpec((1, HEAD_DIM), const),
            pl.BlockSpec((1, HEAD_DIM), const),
            pl.BlockSpec((1, HEAD_DIM), const),
            pl.BlockSpec((V_DIM, 1), const),
        ],
        out_specs=pl.BlockSpec((1, tq, V_DIM), lambda b, h, p, qi, kj: (b, qi[p], h)),
        scratch_shapes=[pltpu.VMEM((2, 1, tq), F32), pltpu.VMEM((2, 1, tq), F32), pltpu.VMEM((2, V_DIM, tq), F32)],
    )
    return pl.pallas_call(
        functools.partial(_prompt_attn_body, lam_init=lam_init, q_slab=ATTN_Q_SLAB, key_piece=ATTN_KEY_PIECE),
        grid_spec=grid_spec,
        out_shape=jax.ShapeDtypeStruct((B, S, n_heads * V_DIM), BF16),
        compiler_params=pltpu.CompilerParams(
            dimension_semantics=("parallel", "parallel", "arbitrary"), vmem_limit_bytes=VMEM_LIMIT),
        name="prompt_attn",
    )(qi, kj, qt, k, vt, lq1, lk1, lq2, lk2, subln_g_col)


def _paged_init(m_sc, l_sc, acc_sc):
    m_sc[...] = jnp.full(m_sc.shape, NEG_INF, F32)
    l_sc[...] = jnp.zeros(l_sc.shape, F32)
    acc_sc[...] = jnp.zeros(acc_sc.shape, F32)


def _paged_scores(qbd, k_refs):
    pairs = [jnp.concatenate([r[...].astype(BF16) for r in k_refs[i:i + 2]], axis=1) for i in range(0, len(k_refs), 2)]
    return jnp.concatenate([_dot(qbd, kk) for kk in pairs], axis=1)


def _paged_accumulate(s, t, v_refs, m_sc, l_sc, acc_sc, *, n_heads):
    page = v_refs[0].shape[0] // n_heads
    m_prev = m_sc[t]
    m_new = jnp.maximum(m_prev, jnp.max(s, axis=-1, keepdims=True))
    alpha = jnp.exp(m_prev - m_new)
    pr = jnp.exp(s - m_new)
    l_sc[t] = alpha * l_sc[t] + jnp.sum(pr, axis=-1, keepdims=True)
    prb = pr.astype(BF16)

    def head(h):
        return jnp.concatenate([r[pl.ds(h, page, stride=n_heads), :] for r in v_refs], axis=0).astype(BF16)

    for h in range(0, n_heads, 2):
        cols = slice(h * V_DIM, (h + 2) * V_DIM)
        acc_sc[t, :, cols] = alpha * acc_sc[t, :, cols] + _dot(prb, jnp.concatenate([head(h), head(h + 1)], axis=1))
    m_sc[t] = m_new


def _paged_finish(qbd, kn_ref, vn_ref, lam, sg_ref, o_ref, m_sc, l_sc, acc_sc, y_sc, *, lam_init, n_heads):
    n_streams, n_rows, _ = acc_sc.shape
    n_new = kn_ref.shape[1]
    half = n_rows // 2
    qf = qbd.astype(F32)
    q_tok = lax.broadcasted_iota(jnp.int32, (n_rows, 1), 0) % n_new
    s_new = []
    for j in range(n_new):
        kj = kn_ref[0, j:j + 1, :].astype(BF16).astype(F32)
        s_new.append(jnp.where(q_tok >= j, jnp.sum(qf * kj, axis=-1, keepdims=True), NEG_INF))
    m_fin = m_sc[0]
    for t in range(1, n_streams):
        m_fin = jnp.maximum(m_fin, m_sc[t])
    for sj in s_new:
        m_fin = jnp.maximum(m_fin, sj)
    l_fin = jnp.zeros_like(m_fin)
    acc = jnp.zeros(acc_sc.shape[1:], F32)
    for t in range(n_streams):
        alpha = jnp.exp(m_sc[t] - m_fin)
        l_fin = l_fin + alpha * l_sc[t]
        acc = acc + alpha * acc_sc[t]
    for j, sj in enumerate(s_new):
        pj = jnp.exp(sj - m_fin)
        l_fin = l_fin + pj
        acc = acc + pj.astype(BF16).astype(F32) * vn_ref[0, j:j + 1, :].astype(BF16).astype(F32)
    o = acc / l_fin
    o = o[:half] - lam * o[half:]
    for h in range(n_heads):
        blk = o[:, h * V_DIM:(h + 1) * V_DIM]
        y_sc[:, h * V_DIM:(h + 1) * V_DIM] = _rms(blk, sg_ref[...]) * (1.0 - lam_init)
    for h in range(n_heads):
        o_ref[0, :, h * V_DIM:(h + 1) * V_DIM] = y_sc[h * n_new:(h + 1) * n_new, h * V_DIM:(h + 1) * V_DIM]


def _mix_out_prompt_body(x_ref, o_ref, u_ref, vn_ref, ws_ref, bs_ref, wout_ref, gpost_ref, y_ref, gate_sc):
    tm = x_ref.shape[0]
    attn_w = o_ref.shape[1]
    gd = vn_ref.shape[1] // N_GROUPS
    row = lax.broadcasted_iota(jnp.int32, (CHUNK, CHUNK), 0)
    col = lax.broadcasted_iota(jnp.int32, (CHUNK, CHUNK), 1)
    for g in range(N_GROUPS):
        wg = jnp.where(col <= row, ws_ref[g], 0.0).astype(BF16)
        bias = bs_ref[:, g * gd:(g + 1) * gd]
        for c in range(tm // CHUNK):
            rows = slice(c * CHUNK, (c + 1) * CHUNK)
            cols = slice(g * gd, (g + 1) * gd)
            s = _dot(wg, vn_ref[rows, cols]) + bias
            gate_sc[rows, cols] = (u_ref[rows, cols] * s).astype(BF16)
    m = _dot(o_ref[...], wout_ref[:attn_w, :]) + _dot(gate_sc[...], wout_ref[attn_w:, :])
    y_ref[...] = x_ref[...] + _rms(m, gpost_ref[...])


def _mix_out_prompt(x2d, o2d, u2d, vn2d, w_spatial, bs_exp, w_out_bf, g_post, *, tm):
    n_rows, d_model = x2d.shape
    attn_w = o2d.shape[1]
    gm_w = u2d.shape[1]
    const2 = lambda i: (0, 0)
    rows = lambda i: (i, 0)
    return pl.pallas_call(
        _mix_out_prompt_body,
        grid=(n_rows // tm,),
        in_specs=[
            pl.BlockSpec((tm, d_model), rows),
            pl.BlockSpec((tm, attn_w), rows),
            pl.BlockSpec((tm, gm_w), rows),
            pl.BlockSpec((tm, gm_w), rows),
            pl.BlockSpec(w_spatial.shape, lambda i: (0, 0, 0)),
            pl.BlockSpec(bs_exp.shape, const2),
            pl.BlockSpec(w_out_bf.shape, const2),
            pl.BlockSpec((1, d_model), const2),
        ],
        out_specs=pl.BlockSpec((tm, d_model), rows),
        out_shape=jax.ShapeDtypeStruct((n_rows, d_model), F32),
        scratch_shapes=[pltpu.VMEM((tm, gm_w), BF16)],
        compiler_params=pltpu.CompilerParams(dimension_semantics=("parallel",),
                                             vmem_limit_bytes=VMEM_LIMIT),
        name="mix_out_prompt",
    )(x2d, o2d, u2d, vn2d, w_spatial, bs_exp, w_out_bf, g_post)


def _mix_out_sample_body(x_ref, o_ref, u_ref, vn_ref, coef_ref, bias_ref, wout_ref, gpost_ref, y_ref, hist):
    t = pl.program_id(0)
    n_t = hist.shape[0]
    attn_w = o_ref.shape[1]

    @pl.when(t == 0)
    def _():
        hist[...] = jnp.zeros(hist.shape, F32)

    hist[t] = vn_ref[...]
    s = bias_ref[pl.ds(t, 1), :]
    for j in range(n_t):
        cj = jnp.where(j <= t, coef_ref[pl.ds(t * n_t + j, 1), :], 0.0)
        s = s + cj.astype(BF16).astype(F32) * hist[j].astype(BF16).astype(F32)
    gate = (u_ref[...] * s).astype(BF16)
    m = _dot(o_ref[...].astype(BF16), wout_ref[:attn_w, :]) + _dot(gate, wout_ref[attn_w:, :])
    y_ref[...] = x_ref[...] + _rms(m, gpost_ref[...])


def _mix_out_sample(x2d, o2d, u2d, vn2d, coef, bias, w_out_bf, g_post, *, n_t):
    n_dec = x2d.shape[0]
    d_model = x2d.shape[1] // n_t
    attn_w = o2d.shape[1] // n_t
    gm_w = u2d.shape[1] // n_t
    const2 = lambda t: (0, 0)
    colblk = lambda t: (0, t)
    return pl.pallas_call(
        _mix_out_sample_body,
        grid=(n_t,),
        in_specs=[
            pl.BlockSpec((n_dec, d_model), colblk),
            pl.BlockSpec((n_dec, attn_w), colblk),
            pl.BlockSpec((n_dec, gm_w), colblk),
            pl.BlockSpec((n_dec, gm_w), colblk),
            pl.BlockSpec(coef.shape, const2),
            pl.BlockSpec(bias.shape, const2),
            pl.BlockSpec(w_out_bf.shape, const2),
            pl.BlockSpec((1, d_model), const2),
        ],
        out_specs=pl.BlockSpec((n_dec, d_model), colblk),
        out_shape=jax.ShapeDtypeStruct(x2d.shape, F32),
        scratch_shapes=[pltpu.VMEM((n_t, n_dec, gm_w), F32)],
        compiler_params=pltpu.CompilerParams(dimension_semantics=("arbitrary",),
                                             vmem_limit_bytes=VMEM_LIMIT),
        name="mix_out_sample",
    )(x2d, o2d, u2d, vn2d, coef, bias, w_out_bf, g_post)


def _ffn_chunks(h, taps, cw_ref, cb_ref, wup_ref, wdn_ref, *, d_ff, col_chunk, chunk_ids, between=()):
    gate_cols = lambda c: slice(c * col_chunk, (c + 1) * col_chunk)
    val_cols = lambda c: slice(d_ff + c * col_chunk, d_ff + (c + 1) * col_chunk)

    def up(c):
        return _dot(h, wup_ref[:, gate_cols(c)]), _dot(h, wup_ref[:, val_cols(c)])

    def conv(u, cols):
        shifted = taps(u, cols)
        y = cb_ref[:, cols] + cw_ref[0:1, cols] * shifted[0]
        for j in range(1, CONV_W):
            y = y + cw_ref[j:j + 1, cols] * shifted[j]
        return y

    def down(a, c):
        return _dot(a, wdn_ref[gate_cols(c), :])

    acc = None
    act = None
    up_next = up(chunk_ids[0])
    for i, c in enumerate(chunk_ids):
        up_gate, up_val = up_next
        if i + 1 < len(chunk_ids):
            up_next = up(chunk_ids[i + 1])
        if act is not None:
            part = down(act, chunk_ids[i - 1])
            acc = part if acc is None else acc + part
        act = (_gelu(conv(up_gate, gate_cols(c))) * conv(up_val, val_cols(c))).astype(BF16)
        if i < len(between):
            between[i]()
    part = down(act, chunk_ids[-1])
    return part if acc is None else acc + part


def _ffn_prompt_body(pt_ref, x_ref, g_ref, wup_ref, cw_ref, cb_ref, wdn_ref, gpost_ref,
                     qbd_ref, kn_ref, vn_ref, lq1_ref, lk1_ref, lq2_ref, lk2_ref, sg_ref, kt_hbm, v_hbm,
                     y_ref, conv_ref, o_ref,
                     carry1, carry2, h_sc, acc_ffn, m_sc, l_sc, acc_sc, y_sc, kbuf, vbuf, sem, *,
                     d_ff, col_chunk, lam_init, n_pages_step, n_streams, n_heads, chunk_groups):
    b = pl.program_id(0)
    j = pl.program_id(1)
    sub = pl.program_id(2)
    n_sub = len(chunk_groups)
    step = (b * pl.num_programs(1) + j) * n_sub + sub
    last_step = pl.num_programs(0) * pl.num_programs(1) * n_sub - 1
    slot = step % 2
    per = n_pages_step // n_streams
    row = lax.broadcasted_iota(jnp.int32, (SUBLANES, 1), 0)

    def page_copies(of_step, into_slot):
        seq = of_step // n_sub
        first_page = (of_step % n_sub) * n_pages_step
        copies = []
        for p in range(n_pages_step):
            phys = pt_ref[seq, first_page + p]
            copies.append(pltpu.make_async_copy(kt_hbm.at[phys], kbuf.at[into_slot, p], sem.at[into_slot, 0]))
            copies.append(pltpu.make_async_copy(v_hbm.at[phys], vbuf.at[into_slot, p], sem.at[into_slot, 1]))
        return copies

    def wait_slot(which):
        pltpu.make_async_copy(kbuf.at[which], kbuf.at[which], sem.at[which, 0]).wait()
        pltpu.make_async_copy(vbuf.at[which], vbuf.at[which], sem.at[which, 1]).wait()

    @pl.when(step == 0)
    def _():
        for c in page_copies(step, slot):
            c.start()

    k_refs = [kbuf.at[slot, p] for p in range(n_pages_step)]
    v_refs = [vbuf.at[slot, p] for p in range(n_pages_step)]

    @pl.when((j == 0) & (sub == 0))
    def _():
        carry1[...] = jnp.zeros(carry1.shape, F32)
        carry2[...] = jnp.zeros(carry2.shape, F32)

    def taps(up, cols):
        r1 = pltpu.roll(up, 1, 0)
        r2 = pltpu.roll(up, 2, 0)
        t1 = jnp.concatenate([jnp.where(row < 1, carry1[:, cols], r1[:SUBLANES]), r1[SUBLANES:]], axis=0)
        t2 = jnp.concatenate([jnp.where(row < 2, carry2[:, cols], r2[:SUBLANES]), r2[SUBLANES:]], axis=0)
        carry1[:, cols] = r1[:SUBLANES]
        carry2[:, cols] = r2[:SUBLANES]
        conv_ref[0, :, cols] = r2[:CONV_W - 1]
        return [t2, t1, up]

    def sub_step(grp, chunk_ids):
        first, last = grp == 0, grp == len(chunk_groups) - 1
        for c in page_copies(jnp.minimum(step + 1, last_step), 1 - slot):
            c.start()
        qbd = qbd_ref[0]
        if first:
            h = _rms(x_ref[0], g_ref[...]).astype(BF16)
            h_sc[...] = h
            _paged_init(m_sc, l_sc, acc_sc)
        else:
            h = h_sc[...]
        wait_slot(slot)
        scores = [None] * n_streams
        scores[0] = _paged_scores(qbd, k_refs[:per])

        def stream(t):
            def run():
                _paged_accumulate(scores[t], t, v_refs[t * per:(t + 1) * per], m_sc, l_sc, acc_sc, n_heads=n_heads)
                if t + 1 < n_streams:
                    scores[t + 1] = _paged_scores(qbd, k_refs[(t + 1) * per:(t + 2) * per])
            return run

        n_between = min(n_streams, len(chunk_ids))
        part = _ffn_chunks(h, taps, cw_ref, cb_ref, wup_ref, wdn_ref, d_ff=d_ff, col_chunk=col_chunk,
                           chunk_ids=chunk_ids, between=[stream(t) for t in range(n_between)])
        for t in range(n_between, n_streams):
            stream(t)()
        if first:
            acc_ffn[...] = part
        elif not last:
            acc_ffn[...] = acc_ffn[...] + part
        if last:
            total = part if first else acc_ffn[...] + part
            y_ref[0] = x_ref[0] + _rms(total, gpost_ref[...])
            lam = _diff_lambda(lq1_ref[...], lk1_ref[...], lq2_ref[...], lk2_ref[...], lam_init)
            _paged_finish(qbd, kn_ref, vn_ref, lam, sg_ref, o_ref, m_sc, l_sc, acc_sc, y_sc,
                          lam_init=lam_init, n_heads=n_heads)

    for grp, chunk_ids in enumerate(chunk_groups):
        pl.when(sub == grp)(functools.partial(sub_step, grp, chunk_ids))

    @pl.when(step == last_step)
    def _():
        wait_slot(1 - slot)


def _ffn_prompt_paged(x3d, g_pre, w_up_bf, conv_w, conv_b, w_down_bf, g_post,
                      page_table, qbd, k_new, v_new, cache_kt, cache_v, lq1, lk1, lq2, lk2, subln_g, *,
                      tm, col_chunk, lam_init, n_heads, n_pages_step, n_streams):
    B, S, d_model = x3d.shape
    d_ff = w_down_bf.shape[0]
    n_dec, n_pages = page_table.shape
    _, width, page = cache_kt.shape
    n_new = k_new.shape[1]
    n_rows = qbd.shape[1]
    G = n_pages_step
    tiles = S // tm
    n_sub = n_pages // G
    n_chunks = d_ff // col_chunk
    assert B * tiles == n_dec and n_pages % G == 0 and G % (2 * n_streams) == 0 and n_sub <= n_chunks
    assert n_heads % 2 == 0
    bounds = [round(i * n_chunks / n_sub) for i in range(n_sub + 1)]
    chunk_groups = tuple(tuple(range(bounds[i], bounds[i + 1])) for i in range(n_sub))

    const2 = lambda b, j, s, pt: (0, 0)
    tile = lambda b, j, s, pt: (b, j, 0)
    per_seq = lambda b, j, s, pt: (b * tiles + j, 0, 0)

    grid_spec = pltpu.PrefetchScalarGridSpec(
        num_scalar_prefetch=1,
        grid=(B, tiles, n_sub),
        in_specs=[
            pl.BlockSpec((1, tm, d_model), tile),
            pl.BlockSpec((1, d_model), const2),
            pl.BlockSpec(w_up_bf.shape, const2, pipeline_mode=pl.Buffered(1)),
            pl.BlockSpec(conv_w.shape, const2),
            pl.BlockSpec(conv_b.shape, const2),
            pl.BlockSpec(w_down_bf.shape, const2, pipeline_mode=pl.Buffered(1)),
            pl.BlockSpec((1, d_model), const2),
            pl.BlockSpec((1, n_rows, width), per_seq),
            pl.BlockSpec((1, n_new, width), per_seq),
            pl.BlockSpec((1, n_new, width), per_seq),
            pl.BlockSpec((1, HEAD_DIM), const2),
            pl.BlockSpec((1, HEAD_DIM), const2),
            pl.BlockSpec((1, HEAD_DIM), const2),
            pl.BlockSpec((1, HEAD_DIM), const2),
            pl.BlockSpec((1, V_DIM), const2),
            pl.BlockSpec(memory_space=pl.ANY),
            pl.BlockSpec(memory_space=pl.ANY),
        ],
        out_specs=[
            pl.BlockSpec((1, tm, d_model), tile),
            pl.BlockSpec((1, CONV_W - 1, 2 * d_ff), lambda b, j, s, pt: (b, 0, 0)),
            pl.BlockSpec((1, n_new, width), per_seq),
        ],
        scratch_shapes=[
            pltpu.VMEM((SUBLANES, 2 * d_ff), F32), pltpu.VMEM((SUBLANES, 2 * d_ff), F32),
            pltpu.VMEM((tm, d_model), BF16), pltpu.VMEM((tm, d_model), F32),
            pltpu.VMEM((n_streams, n_rows, 1), F32), pltpu.VMEM((n_streams, n_rows, 1), F32),
            pltpu.VMEM((n_streams, n_rows, width), F32), pltpu.VMEM((n_rows // 2, width), F32),
            pltpu.VMEM((2, G, width, page), F32), pltpu.VMEM((2, G, page * n_heads, V_DIM), F32),
            pltpu.SemaphoreType.DMA((2, 2)),
        ],
    )
    return pl.pallas_call(
        functools.partial(_ffn_prompt_body, d_ff=d_ff, col_chunk=col_chunk, lam_init=lam_init, n_pages_step=G,
                          n_streams=n_streams, n_heads=n_heads, chunk_groups=chunk_groups),
        grid_spec=grid_spec,
        out_shape=[
            jax.ShapeDtypeStruct((B, S, d_model), F32),
            jax.ShapeDtypeStruct((B, CONV_W - 1, 2 * d_ff), F32),
            jax.ShapeDtypeStruct((n_dec, n_new, width), F32),
        ],
        compiler_params=pltpu.CompilerParams(dimension_semantics=("arbitrary", "arbitrary", "arbitrary"),
                                             vmem_limit_bytes=VMEM_LIMIT_FUSED),
        name="ffn_prompt_paged",
    )(page_table, x3d, g_pre, w_up_bf, conv_w, conv_b, w_down_bf, g_post,
      qbd, k_new, v_new, lq1, lk1, lq2, lk2, subln_g, cache_kt, cache_v)


def _ffn_sample_body(x_ref, st_ref, g_ref, wup_ref, cw_ref, cb_ref, wdn_ref, gpost_ref, y_ref, conv_ref, ring, *,
                     d_ff, col_chunk):
    t = pl.program_id(0)
    width = 2 * d_ff

    @pl.when(t == 0)
    def _():
        for j in range(CONV_W - 1):
            ring[j] = st_ref[:, j * width:(j + 1) * width]

    x = x_ref[...]
    h = _rms(x, g_ref[...]).astype(BF16)
    cur = (t + CONV_W - 1) % CONV_W

    def taps(up, cols):
        ring[cur, :, cols] = up
        conv_ref[:, cols] = up
        return [ring[(t + j) % CONV_W, :, cols] for j in range(CONV_W - 1)] + [up]

    acc = _ffn_chunks(h, taps, cw_ref, cb_ref, wup_ref, wdn_ref, d_ff=d_ff, col_chunk=col_chunk,
                      chunk_ids=tuple(range(d_ff // col_chunk)))
    y_ref[...] = x + _rms(acc, gpost_ref[...])


def _ffn_sample(x2d, state2d, g_pre, w_up_bf, conv_w, conv_b, w_down_bf, g_post, *, n_t, col_chunk):
    n_dec = x2d.shape[0]
    d_model = x2d.shape[1] // n_t
    d_ff = w_down_bf.shape[0]
    width = 2 * d_ff
    const2 = lambda t: (0, 0)
    colblk = lambda t: (0, t)
    conv_map = lambda t: (0, jnp.maximum(t - (n_t - (CONV_W - 1)), 0))
    return pl.pallas_call(
        functools.partial(_ffn_sample_body, d_ff=d_ff, col_chunk=col_chunk),
        grid=(n_t,),
        in_specs=[
            pl.BlockSpec((n_dec, d_model), colblk),
            pl.BlockSpec(state2d.shape, const2),
            pl.BlockSpec((1, d_model), const2),
            pl.BlockSpec(w_up_bf.shape, const2, pipeline_mode=pl.Buffered(1)),
            pl.BlockSpec(conv_w.shape, const2),
            pl.BlockSpec(conv_b.shape, const2),
            pl.BlockSpec(w_down_bf.shape, const2, pipeline_mode=pl.Buffered(1)),
            pl.BlockSpec((1, d_model), const2),
        ],
        out_specs=[
            pl.BlockSpec((n_dec, d_model), colblk),
            pl.BlockSpec((n_dec, width), conv_map),
        ],
        out_shape=[
            jax.ShapeDtypeStruct(x2d.shape, F32),
            jax.ShapeDtypeStruct((n_dec, (CONV_W - 1) * width), F32),
        ],
        scratch_shapes=[pltpu.VMEM((CONV_W, n_dec, width), F32)],
        compiler_params=pltpu.CompilerParams(dimension_semantics=("arbitrary",),
                                             vmem_limit_bytes=VMEM_LIMIT),
        name="ffn_sample",
    )(x2d, state2d, g_pre, w_up_bf, conv_w, conv_b, w_down_bf, g_post)


def _rope_tables(pos):
    half = HEAD_DIM // 2
    inv = ROPE_THETA ** (-jnp.arange(half, dtype=F32) * 2.0 / HEAD_DIM)
    ang = pos[:, None] * inv[None, :]
    cos = jnp.cos(ang)
    sin = jnp.sin(ang)
    cos_t = jnp.concatenate([cos, cos, cos, cos], axis=1)
    sin_t = jnp.concatenate([-sin, sin, -sin, sin], axis=1)
    return cos_t, sin_t


def kernel(x_prompt, x_sample, cache_k, cache_v, state_conv, page_table, norm_mix_pre, w_in, lambda_q1,
           lambda_k1, lambda_q2, lambda_k2, subln_g, gate_ln_g, gate_ln_b, w_spatial, b_spatial, w_out,
           norm_mix_post, norm_ffn_pre, w_up, conv_w, conv_b, w_down, norm_ffn_post):
    B, S, d_model = x_prompt.shape
    n_dec, n_t, _ = x_sample.shape
    depth, n_phys, page, n_heads, _, _ = cache_k.shape
    qk_w = n_heads * 2 * HEAD_DIM
    attn_w = n_heads * V_DIM
    gm_w = (w_in.shape[2] - 2 * qk_w - attn_w) // 2
    d_ff = w_down.shape[1]
    past_len = page_table.shape[1] * page
    assert depth == 1 and w_spatial.shape[1] == N_GROUPS and w_spatial.shape[2] == CHUNK
    assert CONV_W - 1 <= n_t <= CHUNK and conv_w.shape[1] == CONV_W
    widths = dict(qk_w=qk_w, attn_w=attn_w, gm_w=gm_w)

    l = 0
    lam_init = 0.8 - 0.6 * math.exp(-0.3 * l)
    row = lambda a: a[l].reshape(1, -1)
    w_in_bf = w_in[l].astype(BF16)
    w_out_bf = w_out[l].astype(BF16)
    w_up_bf = w_up[l].astype(BF16)
    w_down_bf = w_down[l].astype(BF16)
    lq1, lk1, lq2, lk2 = row(lambda_q1), row(lambda_k1), row(lambda_q2), row(lambda_k2)
    sg = row(subln_g)
    g_mix_pre, g_mix_post = row(norm_mix_pre), row(norm_mix_post)
    g_ffn_pre, g_ffn_post = row(norm_ffn_pre), row(norm_ffn_post)
    ln_g, ln_b = row(gate_ln_g), row(gate_ln_b)
    conv_b2 = row(conv_b)
    gd = gm_w // N_GROUPS

    cos_p, sin_p = _rope_tables(jnp.arange(S, dtype=F32))
    qt_p, kt_p, kb_p, vf_p, vt_p, u_p, vn_p = _mix_in_prompt(
        x_prompt, g_mix_pre, w_in_bf, cos_p, sin_p, ln_g, ln_b, tm=TILE_MIX_IN, **widths)
    o_p = _prompt_attn(qt_p, kb_p, vt_p, lq1, lk1, lq2, lk2, sg.reshape(V_DIM, 1),
                       lam_init=lam_init, n_heads=n_heads, tq=TILE_ATTN)
    bs_exp = jnp.repeat(b_spatial[l].T, gd, axis=1)
    x1_p = _mix_out_prompt(x_prompt.reshape(B * S, d_model), o_p.reshape(B * S, attn_w),
                           u_p.reshape(B * S, gm_w), vn_p.reshape(B * S, gm_w), w_spatial[l], bs_exp, w_out_bf,
                           g_mix_post, tm=TILE_MIX_OUT)
    new_k_p = kt_p.reshape(B, n_heads, 2, HEAD_DIM, S).transpose(0, 4, 1, 2, 3)[None]

    cos_s, sin_s = _rope_tables(jnp.arange(n_t, dtype=F32) + float(past_len))
    cos_s = jnp.repeat(cos_s, n_dec, axis=0)
    sin_s = jnp.repeat(sin_s, n_dec, axis=0)
    xs2d = x_sample.reshape(n_dec, n_t * d_model)
    q_s, kf_s, vf_s, u_s, vn_s = _mix_in_sample(xs2d, g_mix_pre, w_in_bf, cos_s, sin_s, ln_g, ln_b, n_t=n_t,
                                                **widths)
    q_hs = q_s.reshape(n_dec, n_t, n_heads, 2, HEAD_DIM).transpose(0, 3, 2, 1, 4)
    same = ((jnp.arange(2)[:, None, None, None] == jnp.arange(2)[None, None, None, :])
            & (jnp.arange(n_heads)[None, :, None, None] == jnp.arange(n_heads)[None, None, :, None]))
    qbd = jnp.where(same[None, :, :, None, :, :, None], q_hs[:, :, :, :, None, None, :], jnp.zeros((), BF16))
    qbd = qbd.reshape(n_dec, 2 * n_heads * n_t, qk_w)
    cache_kt = cache_k[l].transpose(0, 2, 3, 4, 1).reshape(n_phys, qk_w, page)
    cache_v2 = cache_v[l].reshape(n_phys, page * n_heads, V_DIM)

    y_p, conv_p, o_s = _ffn_prompt_paged(
        x1_p.reshape(B, S, d_model), g_ffn_pre, w_up_bf, conv_w[l], conv_b2, w_down_bf, g_ffn_post,
        page_table, qbd, kf_s.reshape(n_dec, n_t, qk_w), vf_s.reshape(n_dec, n_t, attn_w), cache_kt, cache_v2,
        lq1, lk1, lq2, lk2, sg, tm=TILE_FFN, col_chunk=FFN_COL_CHUNK, lam_init=lam_init, n_heads=n_heads,
        n_pages_step=PAGES_PER_STEP, n_streams=PAGE_STREAMS)

    coef = jnp.repeat(w_spatial[l][:, :n_t, :n_t].transpose(1, 2, 0).reshape(n_t * n_t, N_GROUPS), gd, axis=1)
    bias = jnp.repeat(b_spatial[l][:, :n_t].T, gd, axis=1)
    x1_s = _mix_out_sample(xs2d, o_s.reshape(n_dec, n_t * attn_w), u_s, vn_s, coef, bias, w_out_bf, g_mix_post,
                           n_t=n_t)
    y_s, conv_s = _ffn_sample(x1_s, state_conv[l].reshape(n_dec, (CONV_W - 1) * 2 * d_ff), g_ffn_pre, w_up_bf,
                              conv_w[l], conv_b2, w_down_bf, g_ffn_post, n_t=n_t, col_chunk=FFN_COL_CHUNK)

    return (y_p,
            y_s.reshape(n_dec, n_t, d_model),
            new_k_p,
            vf_p.reshape(1, B, S, n_heads, V_DIM),
            conv_p.reshape(1, B, CONV_W - 1, 2 * d_ff),
            kf_s.reshape(1, n_dec, n_t, n_heads, 2, HEAD_DIM),
            vf_s.reshape(1, n_dec, n_t, n_heads, V_DIM),
            conv_s.reshape(1, n_dec, CONV_W - 1, 2 * d_ff),
            vn_s.reshape(1, n_dec, n_t, gm_w))
```
